```python
import jax, jax.numpy as jnp
from jax import lax
import numpy as np

D_MODEL = 1024
BATCH = 16
SEQ = 2048
DEPTH = 1

HEAD_DIM = 64
ATTN_WIDTH = D_MODEL // 2
CONV_WIDTH = D_MODEL - ATTN_WIDTH
N_Q_HEADS = ATTN_WIDTH // HEAD_DIM
N_KV_HEADS = 2
GQA_GROUP = N_Q_HEADS // N_KV_HEADS
KV_WIDTH = N_KV_HEADS * HEAD_DIM
WINDOW = 128
BLOCK = 128
ROT_DIM = HEAD_DIM // 4
ROPE_THETA = 500000.0
CONV_W = 3
D_FF = 2816
N_MOD = 9
LN_EPS = 1e-5
DN_ALPHA = (2.0 * DEPTH) ** 0.25
DN_BETA = (8.0 * DEPTH) ** -0.25
IN_WIDTH = ATTN_WIDTH + 2 * KV_WIDTH + 3 * CONV_WIDTH

kernel_name = "hybrid_swa_sink_shortconv_macaron_deepnorm_adaln"


def layer_norm(x, g, b):
    xf = x.astype(jnp.float32)
    mu = jnp.mean(xf, axis=-1, keepdims=True)
    var = jnp.mean(jnp.square(xf - mu), axis=-1, keepdims=True)
    y = (xf - mu) * lax.rsqrt(var + LN_EPS) * g.astype(jnp.float32) + b.astype(jnp.float32)
    return y.astype(x.dtype)


def swiglu(h, w_gate_up, w_down):
    gu = h @ w_gate_up
    g, u = jnp.split(gu, 2, axis=-1)
    return (jax.nn.silu(g) * u) @ w_down


def partial_rope(x, positions):
    half = ROT_DIM // 2
    inv_freq = jnp.power(jnp.float32(ROPE_THETA), -jnp.arange(0, ROT_DIM, 2, dtype=jnp.float32) / ROT_DIM)
    ang = positions.astype(jnp.float32)[..., None] * inv_freq
    cos = jnp.cos(ang)[:, :, None, :].astype(x.dtype)
    sin = jnp.sin(ang)[:, :, None, :].astype(x.dtype)
    x1 = x[..., :half]
    x2 = x[..., half:ROT_DIM]
    rest = x[..., ROT_DIM:]
    return jnp.concatenate([x1 * cos - x2 * sin, x2 * cos + x1 * sin, rest], axis=-1)


def sliding_window_sink_attention(q, k, v, sinks):
    bsz, seq = q.shape[0], q.shape[1]
    nb = seq // BLOCK
    qb = q.reshape(bsz, nb, BLOCK, N_KV_HEADS, GQA_GROUP, HEAD_DIM)
    pad = ((0, 0), (BLOCK, 0), (0, 0), (0, 0))
    kp = jnp.pad(k, pad).reshape(bsz, nb + 1, BLOCK, N_KV_HEADS, HEAD_DIM)
    vp = jnp.pad(v, pad).reshape(bsz, nb + 1, BLOCK, N_KV_HEADS, HEAD_DIM)
    kb = jnp.concatenate([kp[:, :-1], kp[:, 1:]], axis=2)
    vb = jnp.concatenate([vp[:, :-1], vp[:, 1:]], axis=2)
    scale = HEAD_DIM ** -0.5
    scores = jnp.einsum('bnqhgd,bnkhd->bnhgqk', qb, kb).astype(jnp.float32) * scale
    blk = jnp.arange(nb)[:, None, None]
    qi = jnp.arange(BLOCK)[None, :, None]
    ki = jnp.arange(2 * BLOCK)[None, None, :]
    diff = qi + BLOCK - ki
    key_pos = (blk - 1) * BLOCK + ki
    valid = (diff >= 0) & (diff < WINDOW) & (key_pos >= 0)
    scores = jnp.where(valid[None, :, None, None], scores, -jnp.inf)
    sink = jnp.broadcast_to(
        sinks.astype(jnp.float32).reshape(N_KV_HEADS, GQA_GROUP)[None, None, :, :, None, None],
        scores.shape[:-1] + (1,))
    probs = jax.nn.softmax(jnp.concatenate([scores, sink], axis=-1), axis=-1)[..., :-1]
    out = jnp.einsum('bnhgqk,bnkhd->bnqhgd', probs.astype(v.dtype), vb)
    return out.reshape(bsz, seq, N_Q_HEADS * HEAD_DIM)


def short_gated_conv(u, b_gate, c_gate, conv_w):
    seq = u.shape[1]
    z = c_gate * u
    zp = jnp.pad(z, ((0, 0), (CONV_W - 1, 0), (0, 0)))
    y = conv_w[0] * zp[:, 0:seq]
    for tap in range(1, CONV_W):
        y = y + conv_w[tap] * zp[:, tap:tap + seq]
    return b_gate * y


def setup_inputs(seed: int = 0) -> dict:
    key = jax.random.key(seed)
    ks = jax.random.split(key, 20)
    nrm = lambda k, shape, s: jax.random.normal(k, shape, jnp.float32) * s
    L, D = DEPTH, D_MODEL
    x = jax.random.normal(ks[0], (BATCH, SEQ, D), jnp.float32)
    c = jax.random.normal(ks[1], (BATCH, D), jnp.float32)
    offsets = jax.random.randint(ks[2], (BATCH, 1), 0, 1024, dtype=jnp.int32)
    positions = offsets + jnp.arange(SEQ, dtype=jnp.int32)[None, :]
    return {
        "x": x,
        "c": c,
        "positions": positions,
        "w_ada": nrm(ks[3], (L, D, N_MOD * D), 0.1 * D ** -0.5),
        "b_ada": nrm(ks[4], (L, N_MOD * D), 0.01),
        "ffn1_w_gate_up": nrm(ks[5], (L, D, 2 * D_FF), D ** -0.5),
        "ffn1_w_down": nrm(ks[6], (L, D_FF, D), DN_BETA * D_FF ** -0.5),
        "ln1_g": 1.0 + nrm(ks[7], (L, D), 0.02),
        "ln1_b": nrm(ks[8], (L, D), 0.02),
        "w_in": nrm(ks[9], (L, D, IN_WIDTH), D ** -0.5),
        "conv_w": nrm(ks[10], (L, CONV_W, CONV_WIDTH), CONV_W ** -0.5),
        "attn_sinks": nrm(ks[11], (L, N_Q_HEADS), 1.0),
        "w_out": nrm(ks[12], (L, D, D), DN_BETA * D ** -0.5),
        "ln2_g": 1.0 + nrm(ks[13], (L, D), 0.02),
        "ln2_b": nrm(ks[14], (L, D), 0.02),
        "ffn2_w_gate_up": nrm(ks[15], (L, D, 2 * D_FF), D ** -0.5),
        "ffn2_w_down": nrm(ks[16], (L, D_FF, D), DN_BETA * D_FF ** -0.5),
        "ln3_g": 1.0 + nrm(ks[17], (L, D), 0.02),
        "ln3_b": nrm(ks[18], (L, D), 0.02),
    }


def reference(x, c, positions, w_ada, b_ada, ffn1_w_gate_up, ffn1_w_down, ln1_g, ln1_b,
              w_in, conv_w, attn_sinks, w_out, ln2_g, ln2_b,
              ffn2_w_gate_up, ffn2_w_down, ln3_g, ln3_b):
    bsz, seq = x.shape[0], x.shape[1]
    split_at = [ATTN_WIDTH, ATTN_WIDTH + KV_WIDTH, ATTN_WIDTH + 2 * KV_WIDTH,
                ATTN_WIDTH + 2 * KV_WIDTH + CONV_WIDTH, ATTN_WIDTH + 2 * KV_WIDTH + 2 * CONV_WIDTH]
    cond = jax.nn.silu(c)
    for l in range(DEPTH):
        mod = (cond @ w_ada[l] + b_ada[l])[:, None, :]
        sh1, sc1, g1, sh2, sc2, g2, sh3, sc3, g3 = jnp.split(mod, N_MOD, axis=-1)

        h = x * (1 + sc1) + sh1
        x = layer_norm(DN_ALPHA * x + 0.5 * (1 + g1) * swiglu(h, ffn1_w_gate_up[l], ffn1_w_down[l]),
                       ln1_g[l], ln1_b[l])

        h = x * (1 + sc2) + sh2
        proj = h @ w_in[l]
        q, k, v, u, b_gate, c_gate = jnp.split(proj, split_at, axis=-1)
        q = partial_rope(q.reshape(bsz, seq, N_Q_HEADS, HEAD_DIM), positions)
        k = partial_rope(k.reshape(bsz, seq, N_KV_HEADS, HEAD_DIM), positions)
        v = v.reshape(bsz, seq, N_KV_HEADS, HEAD_DIM)
        attn_out = sliding_window_sink_attention(q, k, v, attn_sinks[l])
        conv_out = short_gated_conv(u, b_gate, c_gate, conv_w[l])
        mix = jnp.concatenate([attn_out, conv_out], axis=-1) @ w_out[l]
        x = layer_norm(DN_ALPHA * x + (1 + g2) * mix, ln2_g[l], ln2_b[l])

        h = x * (1 + sc3) + sh3
        x = layer_norm(DN_ALPHA * x + 0.5 * (1 + g3) * swiglu(h, ffn2_w_gate_up[l], ffn2_w_down[l]),
                       ln3_g[l], ln3_b[l])
    return x
```

```python
import functools

import jax
import jax.numpy as jnp
from jax import lax
from jax.experimental import pallas as pl
from jax.experimental.pallas import tpu as pltpu

D_MODEL = 1024
HEAD_DIM = 64
ATTN_WIDTH = 512
CONV_WIDTH = 512
N_Q_HEADS = 8
N_KV_HEADS = 2
KV_WIDTH = N_KV_HEADS * HEAD_DIM
WINDOW = 128
ROT_DIM = 16
ROPE_THETA = 500000.0
CONV_W = 3
D_FF = 2816
N_MOD = 9
LN_EPS = 1e-5
IN_WIDTH = ATTN_WIDTH + 2 * KV_WIDTH + 3 * CONV_WIDTH

LANES = 128
SUBLANES = 8
VMEM_LIMIT_BYTES = 56 * 1024 * 1024

FFN_ROWS = 512
FFN_CHUNK = 256
MIX_ROWS = 512
NEG_BIG = -1e30

F32 = jnp.float32
BF16 = jnp.bfloat16


def _layer_norm(y, g, b):
    mu = jnp.mean(y, axis=-1, keepdims=True)
    d = y - mu
    var = jnp.mean(d * d, axis=-1, keepdims=True)
    return d * lax.rsqrt(var + LN_EPS) * g + b


def _silu(x):
    return x * jax.nn.sigmoid(x)


def _ada_kernel(c_ref, w_ref, b_ref, o_ref):
    cond = _silu(c_ref[...]).astype(BF16)
    o_ref[...] = jnp.dot(cond, w_ref[...].astype(BF16),
                         preferred_element_type=F32) + b_ref[...]


def _ada_mod(c, w_ada, b_ada):
    bsz, d = c.shape
    n = w_ada.shape[1]
    tn = 9 * LANES
    return pl.pallas_call(
        _ada_kernel,
        grid=(n // tn,),
        in_specs=[
            pl.BlockSpec((bsz, d), lambda j: (0, 0)),
            pl.BlockSpec((d, tn), lambda j: (0, j)),
            pl.BlockSpec((1, tn), lambda j: (0, j)),
        ],
        out_specs=pl.BlockSpec((bsz, tn), lambda j: (0, j)),
        out_shape=jax.ShapeDtypeStruct((bsz, n), F32),
        compiler_params=pltpu.CompilerParams(dimension_semantics=("arbitrary",)),
        name="ada_mod",
    )(c, w_ada, b_ada.reshape(1, n))


def _ffn_kernel(x_ref, mod_ref, wgu_ref, wd_ref, g_ref, b_ref, o_ref, h_ref, acc_ref,
                *, mod_base, alpha):
    n_chunks, _, two_tf = wgu_ref.shape
    tf = two_tf // 2
    x = x_ref[0]
    sh = mod_ref[0, mod_base:mod_base + 1, :]
    sc = mod_ref[0, mod_base + 1:mod_base + 2, :]
    gate = mod_ref[0, mod_base + 2:mod_base + 3, :]
    h_ref[...] = (x * (1.0 + sc) + sh).astype(BF16)
    for c in range(n_chunks):
        gu = jnp.dot(h_ref[...], wgu_ref[c], preferred_element_type=F32)
        act = (_silu(gu[:, :tf]) * gu[:, tf:]).astype(BF16)
        d = jnp.dot(act, wd_ref[c], preferred_element_type=F32)
        if c == 0:
            acc_ref[...] = d
        else:
            acc_ref[...] += d
    y = alpha * x + (0.5 * (1.0 + gate)) * acc_ref[...]
    o_ref[0] = _layer_norm(y, g_ref[...], b_ref[...])


def _ffn_block(x, mod, w_gate_up, w_down, ln_g, ln_b, *, mod_base, alpha):
    bsz, seq, d = x.shape
    nc = D_FF // FFN_CHUNK
    wg = w_gate_up[:, :D_FF].reshape(d, nc, FFN_CHUNK)
    wu = w_gate_up[:, D_FF:].reshape(d, nc, FFN_CHUNK)
    wgu = jnp.concatenate([wg, wu], axis=2).astype(BF16).transpose(1, 0, 2)
    wd = w_down.astype(BF16).reshape(nc, FFN_CHUNK, d)
    tm = FFN_ROWS
    const3 = lambda b, s: (0, 0, 0)
    const2 = lambda b, s: (0, 0)
    return pl.pallas_call(
        functools.partial(_ffn_kernel, mod_base=mod_base, alpha=alpha),
        grid=(bsz, seq // tm),
        in_specs=[
            pl.BlockSpec((1, tm, d), lambda b, s: (b, s, 0)),
            pl.BlockSpec((1, N_MOD, d), lambda b, s: (b, 0, 0)),
            pl.BlockSpec((nc, d, 2 * FFN_CHUNK), const3, pipeline_mode=pl.Buffered(1)),
            pl.BlockSpec((nc, FFN_CHUNK, d), const3, pipeline_mode=pl.Buffered(1)),
            pl.BlockSpec((1, d), const2),
            pl.BlockSpec((1, d), const2),
        ],
        out_specs=pl.BlockSpec((1, tm, d), lambda b, s: (b, s, 0)),
        out_shape=jax.ShapeDtypeStruct((bsz, seq, d), F32),
        scratch_shapes=[pltpu.VMEM((tm, d), BF16), pltpu.VMEM((tm, d), F32)],
        compiler_params=pltpu.CompilerParams(
            dimension_semantics=("arbitrary", "arbitrary"),
            vmem_limit_bytes=VMEM_LIMIT_BYTES),
        name="ffn_block",
    )(x, mod, wgu, wd, ln_g.reshape(1, d), ln_b.reshape(1, d))


def _mixer_kernel(sink_ref, x_ref, mod_ref, pos_ref, invf_ref, win_ref, cw_ref, wout_ref,
                  g_ref, b_ref, o_ref,
                  h_ref, proj_ref, q_ref, kvar_ref, vvar_ref, z_ref, mix_ref, *, alpha):
    ts = x_ref.shape[1]
    s_idx = pl.program_id(1)
    zpad = SUBLANES

    @pl.when(s_idx == 0)
    def _():
        kvar_ref[:, 0:WINDOW, :] = jnp.zeros((4, WINDOW, KV_WIDTH), BF16)
        vvar_ref[:, 0:WINDOW, :] = jnp.zeros((4, WINDOW, KV_WIDTH), BF16)
        z_ref[0:zpad, :] = jnp.zeros((zpad, CONV_WIDTH), F32)

    x = x_ref[0]
    sh = mod_ref[0, 3:4, :]
    sc = mod_ref[0, 4:5, :]
    gate = mod_ref[0, 5:6, :]
    h_ref[...] = (x * (1.0 + sc) + sh).astype(BF16)
    proj_ref[...] = jnp.dot(h_ref[...], win_ref[...], preferred_element_type=F32)

    ang = pos_ref[0].astype(F32) * invf_ref[...]
    cos_t = jnp.cos(ang)
    sin_t = jnp.sin(ang)
    lane = lax.broadcasted_iota(jnp.int32, (1, LANES), 1)
    m = lane % HEAD_DIM
    half = ROT_DIM // 2
    sin_up = jnp.where(m < half, -sin_t, 0.0)
    sin_dn = jnp.where((m >= half) & (m < ROT_DIM), sin_t, 0.0)

    def rope(slab):
        up = pltpu.roll(slab, LANES - half, axis=1)
        dn = pltpu.roll(slab, half, axis=1)
        return slab * cos_t + up * sin_up + dn * sin_dn

    scale = HEAD_DIM ** -0.5
    for p in range(ATTN_WIDTH // LANES):
        q_ref[:, p * LANES:(p + 1) * LANES] = (
            rope(proj_ref[:, p * LANES:(p + 1) * LANES]) * scale).astype(BF16)

    low = lane < HEAD_DIM
    k = rope(proj_ref[:, ATTN_WIDTH:ATTN_WIDTH + KV_WIDTH])
    v = proj_ref[:, ATTN_WIDTH + KV_WIDTH:ATTN_WIDTH + 2 * KV_WIDTH]
    for src, dst in ((k, kvar_ref), (v, vvar_ref)):
        swapped = pltpu.roll(src, HEAD_DIM, axis=1)
        dst[0, WINDOW:WINDOW + ts, :] = jnp.where(low, src, 0.0).astype(BF16)
        dst[1, WINDOW:WINDOW + ts, :] = jnp.where(low, 0.0, swapped).astype(BF16)
        dst[2, WINDOW:WINDOW + ts, :] = jnp.where(low, swapped, 0.0).astype(BF16)
        dst[3, WINDOW:WINDOW + ts, :] = jnp.where(low, 0.0, src).astype(BF16)

    rows2 = 2 * WINDOW
    qi = lax.broadcasted_iota(jnp.int32, (rows2, rows2), 0) % WINDOW
    ki = lax.broadcasted_iota(jnp.int32, (rows2, rows2), 1)
    band = (ki > qi) & (ki <= qi + WINDOW)
    first_lo = jnp.where(s_idx > 0, 0, WINDOW)
    top_rows = lax.broadcasted_iota(jnp.int32, (rows2, 1), 0) < WINDOW
    contract_last = (((1,), (1,)), ((), ()))
    for n in range(ts // WINDOW):
        valid = band & (ki >= first_lo) if n == 0 else band
        r0 = n * WINDOW
        for g in range(N_KV_HEADS):
            qs = jnp.concatenate(
                [q_ref[r0:r0 + WINDOW, (2 * g) * LANES:(2 * g + 1) * LANES],
                 q_ref[r0:r0 + WINDOW, (2 * g + 1) * LANES:(2 * g + 2) * LANES]], axis=0)
            out = None
            for hl in range(2):
                var = 2 * g + hl
                keys = kvar_ref[var, r0:r0 + rows2, :]
                vals = vvar_ref[var, r0:r0 + rows2, :]
                sink = jnp.where(top_rows, sink_ref[4 * g + hl], sink_ref[4 * g + 2 + hl])
                sco = lax.dot_general(qs, keys, contract_last, preferred_element_type=F32)
                sco = jnp.where(valid, sco, NEG_BIG)
                mx = jnp.maximum(jnp.max(sco, axis=1, keepdims=True), sink)
                pr = jnp.exp(sco - mx)
                den = jnp.sum(pr, axis=1, keepdims=True) + jnp.exp(sink - mx)
                pv = jnp.dot(pr.astype(BF16), vals, preferred_element_type=F32)
                pv = pv * (1.0 / den)
                out = pv if out is None else out + pv
            mix_ref[r0:r0 + WINDOW, (2 * g) * LANES:(2 * g + 1) * LANES] = out[:WINDOW].astype(BF16)
            mix_ref[r0:r0 + WINDOW, (2 * g + 1) * LANES:(2 * g + 2) * LANES] = out[WINDOW:].astype(BF16)

    c0 = ATTN_WIDTH + 2 * KV_WIDTH
    u = proj_ref[:, c0:c0 + CONV_WIDTH]
    b_gate = proj_ref[:, c0 + CONV_WIDTH:c0 + 2 * CONV_WIDTH]
    c_gate = proj_ref[:, c0 + 2 * CONV_WIDTH:c0 + 3 * CONV_WIDTH]
    z_ref[zpad:zpad + ts, :] = c_gate * u
    y = cw_ref[0:1, :] * z_ref[zpad - 2:zpad - 2 + ts, :]
    y = y + cw_ref[1:2, :] * z_ref[zpad - 1:zpad - 1 + ts, :]
    y = y + cw_ref[2:3, :] * z_ref[zpad:zpad + ts, :]
    mix_ref[:, ATTN_WIDTH:] = (b_gate * y).astype(BF16)

    kvar_ref[:, 0:WINDOW, :] = kvar_ref[:, ts:ts + WINDOW, :]
    vvar_ref[:, 0:WINDOW, :] = vvar_ref[:, ts:ts + WINDOW, :]
    z_ref[0:zpad, :] = z_ref[ts:ts + zpad, :]

    mix = jnp.dot(mix_ref[...], wout_ref[...], preferred_element_type=F32)
    o_ref[0] = _layer_norm(alpha * x + (1.0 + gate) * mix, g_ref[...], b_ref[...])


def _mixer_block(x, mod, positions, w_in, conv_w, sinks, w_out, ln_g, ln_b, *, alpha):
    bsz, seq, d = x.shape
    ts = MIX_ROWS
    inv_freq = jnp.power(jnp.float32(ROPE_THETA),
                         -jnp.arange(0, ROT_DIM, 2, dtype=F32) / ROT_DIM)
    m = jnp.arange(LANES) % HEAD_DIM
    invf = jnp.where(m < ROT_DIM, inv_freq[m % (ROT_DIM // 2)], 0.0).reshape(1, LANES)
    const2 = lambda b, s, sink: (0, 0)
    grid_spec = pltpu.PrefetchScalarGridSpec(
        num_scalar_prefetch=1,
        grid=(bsz, seq // ts),
        in_specs=[
            pl.BlockSpec((1, ts, d), lambda b, s, sink: (b, s, 0)),
            pl.BlockSpec((1, N_MOD, d), lambda b, s, sink: (b, 0, 0)),
            pl.BlockSpec((1, ts, 1), lambda b, s, sink: (b, s, 0)),
            pl.BlockSpec((1, LANES), const2),
            pl.BlockSpec((d, IN_WIDTH), const2, pipeline_mode=pl.Buffered(1)),
            pl.BlockSpec((CONV_W, CONV_WIDTH), const2),
            pl.BlockSpec((d, d), const2, pipeline_mode=pl.Buffered(1)),
            pl.BlockSpec((1, d), const2),
            pl.BlockSpec((1, d), const2),
        ],
        out_specs=pl.BlockSpec((1, ts, d), lambda b, s, sink: (b, s, 0)),
        scratch_shapes=[
            pltpu.VMEM((ts, d), BF16),
            pltpu.VMEM((ts, IN_WIDTH), F32),
            pltpu.VMEM((ts, ATTN_WIDTH), BF16),
            pltpu.VMEM((4, ts + WINDOW, KV_WIDTH), BF16),
            pltpu.VMEM((4, ts + WINDOW, KV_WIDTH), BF16),
            pltpu.VMEM((ts + SUBLANES, CONV_WIDTH), F32),
            pltpu.VMEM((ts, d), BF16),
        ],
    )
    return pl.pallas_call(
        functools.partial(_mixer_kernel, alpha=alpha),
        grid_spec=grid_spec,
        out_shape=jax.ShapeDtypeStruct((bsz, seq, d), F32),
        compiler_params=pltpu.CompilerParams(
            dimension_semantics=("arbitrary", "arbitrary"),
            vmem_limit_bytes=VMEM_LIMIT_BYTES),
        name="mixer_block",
    )(sinks, x, mod, positions.reshape(bsz, seq, 1), invf, w_in.astype(BF16), conv_w,
      w_out.astype(BF16), ln_g.reshape(1, d), ln_b.reshape(1, d))


def kernel(x, c, positions, w_ada, b_ada, ffn1_w_gate_up, ffn1_w_down, ln1_g, ln1_b, w_in, conv_w, attn_sinks, w_out, ln2_g, ln2_b, ffn2_w_gate_up, ffn2_w_down, ln3_g, ln3_b):
    depth = w_ada.shape[0]
    alpha = (2.0 * depth) ** 0.25
    bsz = x.shape[0]
    for l in range(depth):
        mod = _ada_mod(c, w_ada[l], b_ada[l]).reshape(bsz, N_MOD, D_MODEL)
        x = _ffn_block(x, mod, ffn1_w_gate_up[l], ffn1_w_down[l], ln1_g[l], ln1_b[l],
                       mod_base=0, alpha=alpha)
        x = _mixer_block(x, mod, positions, w_in[l], conv_w[l], attn_sinks[l], w_out[l],
                         ln2_g[l], ln2_b[l], alpha=alpha)
        x = _ffn_block(x, mod, ffn2_w_gate_up[l], ffn2_w_down[l], ln3_g[l], ln3_b[l],
                       mod_base=6, alpha=alpha)
    return x
```

```python
import functools

import jax
import jax.numpy as jnp
from jax import lax
from jax.experimental import pallas as pl
from jax.experimental.pallas import tpu as pltpu

D_MODEL = 1024
HEAD_DIM = 64
ATTN_WIDTH = 512
CONV_WIDTH = 512
N_Q_HEADS = 8
N_KV_HEADS = 2
KV_WIDTH = N_KV_HEADS * HEAD_DIM
WINDOW = 128
ROT_DIM = 16
ROPE_THETA = 500000.0
CONV_W = 3
D_FF = 2816
N_MOD = 9
LN_EPS = 1e-5
IN_WIDTH = ATTN_WIDTH + 2 * KV_WIDTH + 3 * CONV_WIDTH

LANES = 128
SUBLANES = 8
VMEM_LIMIT_BYTES = 56 * 1024 * 1024

FFN_ROWS = 1024
FFN_SUB_ROWS = 512
FFN_CHUNK = 256
MIX_ROWS = 512
NEG_BIG = -1e30

F32 = jnp.float32
BF16 = jnp.bfloat16


def _layer_norm(y, g, b):
    mu = jnp.mean(y, axis=-1, keepdims=True)
    d = y - mu
    var = jnp.mean(d * d, axis=-1, keepdims=True)
    return d * lax.rsqrt(var + LN_EPS) * g + b


def _silu(x):
    return x * jax.nn.sigmoid(x)


def _ada_kernel(c_ref, w_ref, b_ref, o_ref):
    cond = _silu(c_ref[...]).astype(BF16)
    o_ref[...] = jnp.dot(cond, w_ref[...].astype(BF16),
                         preferred_element_type=F32) + b_ref[...]


def _ada_mod(c, w_ada, b_ada):
    bsz, d = c.shape
    n = w_ada.shape[1]
    tn = 9 * LANES
    return pl.pallas_call(
        _ada_kernel,
        grid=(n // tn,),
        in_specs=[
            pl.BlockSpec((bsz, d), lambda j: (0, 0)),
            pl.BlockSpec((d, tn), lambda j: (0, j)),
            pl.BlockSpec((1, tn), lambda j: (0, j)),
        ],
        out_specs=pl.BlockSpec((bsz, tn), lambda j: (0, j)),
        out_shape=jax.ShapeDtypeStruct((bsz, n), F32),
        compiler_params=pltpu.CompilerParams(dimension_semantics=("arbitrary",)),
        name="ada_mod",
    )(c, w_ada, b_ada.reshape(1, n))


def _ffn_kernel(x_ref, mod_ref, wgu_ref, wd_ref, g_ref, b_ref, o_ref, *scratch,
                mod_base, alpha):
    d_ff = wd_ref.shape[0]
    tf = FFN_CHUNK
    n_sub = len(scratch) // 2
    sh = mod_ref[0, mod_base:mod_base + 1, :]
    sc = mod_ref[0, mod_base + 1:mod_base + 2, :]
    gate = mod_ref[0, mod_base + 2:mod_base + 3, :]
    for r in range(n_sub):
        h_ref, acc_ref = scratch[2 * r], scratch[2 * r + 1]
        sub = h_ref.shape[0]
        rows = slice(r * sub, (r + 1) * sub)
        x = x_ref[0, rows, :]
        h_ref[...] = (x * (1.0 + sc) + sh).astype(BF16)
        for c in range(d_ff // tf):
            lo = c * tf
            g = jnp.dot(h_ref[...], wgu_ref[:, lo:lo + tf], preferred_element_type=F32)
            u = jnp.dot(h_ref[...], wgu_ref[:, d_ff + lo:d_ff + lo + tf],
                        preferred_element_type=F32)
            act = (_silu(g) * u).astype(BF16)
            d = jnp.dot(act, wd_ref[lo:lo + tf, :], preferred_element_type=F32)
            if c == 0:
                acc_ref[...] = d
            else:
                acc_ref[...] += d
        y = alpha * x + (0.5 * (1.0 + gate)) * acc_ref[...]
        o_ref[0, rows, :] = _layer_norm(y, g_ref[...], b_ref[...])


def _ffn_block(x, mod, w_gate_up, w_down, ln_g, ln_b, *, mod_base, alpha):
    bsz, seq, d = x.shape
    tm = FFN_ROWS
    const2 = lambda b, s: (0, 0)
    return pl.pallas_call(
        functools.partial(_ffn_kernel, mod_base=mod_base, alpha=alpha),
        grid=(bsz, seq // tm),
        in_specs=[
            pl.BlockSpec((1, tm, d), lambda b, s: (b, s, 0)),
            pl.BlockSpec((1, N_MOD, d), lambda b, s: (b, 0, 0)),
            pl.BlockSpec((d, 2 * D_FF), const2, pipeline_mode=pl.Buffered(1)),
            pl.BlockSpec((D_FF, d), const2, pipeline_mode=pl.Buffered(1)),
            pl.BlockSpec((1, d), const2),
            pl.BlockSpec((1, d), const2),
        ],
        out_specs=pl.BlockSpec((1, tm, d), lambda b, s: (b, s, 0)),
        out_shape=jax.ShapeDtypeStruct((bsz, seq, d), F32),
        scratch_shapes=[pltpu.VMEM((FFN_SUB_ROWS, d), dt)
                        for _ in range(tm // FFN_SUB_ROWS) for dt in (BF16, F32)],
        compiler_params=pltpu.CompilerParams(
            dimension_semantics=("arbitrary", "arbitrary"),
            vmem_limit_bytes=VMEM_LIMIT_BYTES),
        name="ffn_block",
    )(x, mod, w_gate_up.astype(BF16), w_down.astype(BF16), ln_g.reshape(1, d), ln_b.reshape(1, d))


def _mixer_kernel(sink_ref, x_ref, mod_ref, pos_ref, invf_ref, win_ref, cw_ref, wout_ref,
                  g_ref, b_ref, o_ref,
                  h_ref, proj_ref, q_ref, kvar_ref, vvar_ref, z_ref, mix_ref, *, alpha):
    ts = x_ref.shape[1]
    s_idx = pl.program_id(1)
    zpad = SUBLANES

    @pl.when(s_idx == 0)
    def _():
        kvar_ref[:, 0:WINDOW, :] = jnp.zeros((4, WINDOW, KV_WIDTH), BF16)
        vvar_ref[:, 0:WINDOW, :] = jnp.zeros((4, WINDOW, KV_WIDTH), BF16)
        z_ref[0:zpad, :] = jnp.zeros((zpad, CONV_WIDTH), F32)

    x = x_ref[0]
    sh = mod_ref[0, 3:4, :]
    sc = mod_ref[0, 4:5, :]
    gate = mod_ref[0, 5:6, :]
    h_ref[...] = (x * (1.0 + sc) + sh).astype(BF16)
    proj_ref[...] = jnp.dot(h_ref[...], win_ref[...], preferred_element_type=F32)

    ang = pos_ref[0].astype(F32) * invf_ref[...]
    cos_t = jnp.cos(ang)
    sin_t = jnp.sin(ang)
    lane = lax.broadcasted_iota(jnp.int32, (1, LANES), 1)
    m = lane % HEAD_DIM
    half = ROT_DIM // 2
    sin_up = jnp.where(m < half, -sin_t, 0.0)
    sin_dn = jnp.where((m >= half) & (m < ROT_DIM), sin_t, 0.0)

    def rope(slab):
        up = pltpu.roll(slab, LANES - half, axis=1)
        dn = pltpu.roll(slab, half, axis=1)
        return slab * cos_t + up * sin_up + dn * sin_dn

    scale = HEAD_DIM ** -0.5
    for p in range(ATTN_WIDTH // LANES):
        q_ref[:, p * LANES:(p + 1) * LANES] = (
            rope(proj_ref[:, p * LANES:(p + 1) * LANES]) * scale).astype(BF16)

    low = lane < HEAD_DIM
    k = rope(proj_ref[:, ATTN_WIDTH:ATTN_WIDTH + KV_WIDTH])
    v = proj_ref[:, ATTN_WIDTH + KV_WIDTH:ATTN_WIDTH + 2 * KV_WIDTH]
    for src, dst in ((k, kvar_ref), (v, vvar_ref)):
        swapped = pltpu.roll(src, HEAD_DIM, axis=1)
        dst[0, WINDOW:WINDOW + ts, :] = jnp.where(low, src, 0.0).astype(BF16)
        dst[1, WINDOW:WINDOW + ts, :] = jnp.where(low, 0.0, swapped).astype(BF16)
        dst[2, WINDOW:WINDOW + ts, :] = jnp.where(low, swapped, 0.0).astype(BF16)
        dst[3, WINDOW:WINDOW + ts, :] = jnp.where(low, 0.0, src).astype(BF16)

    rows2 = 2 * WINDOW
    qi = lax.broadcasted_iota(jnp.int32, (rows2, rows2), 0) % WINDOW
    ki = lax.broadcasted_iota(jnp.int32, (rows2, rows2), 1)
    band = (ki > qi) & (ki <= qi + WINDOW)
    first_lo = jnp.where(s_idx > 0, 0, WINDOW)
    top_rows = lax.broadcasted_iota(jnp.int32, (rows2, 1), 0) < WINDOW
    contract_last = (((1,), (1,)), ((), ()))
    for n in range(ts // WINDOW):
        valid = band & (ki >= first_lo) if n == 0 else band
        r0 = n * WINDOW
        for g in range(N_KV_HEADS):
            qs = jnp.concatenate(
                [q_ref[r0:r0 + WINDOW, (2 * g) * LANES:(2 * g + 1) * LANES],
                 q_ref[r0:r0 + WINDOW, (2 * g + 1) * LANES:(2 * g + 2) * LANES]], axis=0)
            out = None
            for hl in range(2):
                var = 2 * g + hl
                keys = kvar_ref[var, r0:r0 + rows2, :]
                vals = vvar_ref[var, r0:r0 + rows2, :]
                sink = jnp.where(top_rows, sink_ref[4 * g + hl], sink_ref[4 * g + 2 + hl])
                sco = lax.dot_general(qs, keys, contract_last, preferred_element_type=F32)
                sco = jnp.where(valid, sco, NEG_BIG)
                mx = jnp.maximum(jnp.max(sco, axis=1, keepdims=True), sink)
                pr = jnp.exp(sco - mx)
                den = jnp.sum(pr, axis=1, keepdims=True) + jnp.exp(sink - mx)
                pv = jnp.dot(pr.astype(BF16), vals, preferred_element_type=F32)
                pv = pv * (1.0 / den)
                out = pv if out is None else out + pv
            mix_ref[r0:r0 + WINDOW, (2 * g) * LANES:(2 * g + 1) * LANES] = out[:WINDOW].astype(BF16)
            mix_ref[r0:r0 + WINDOW, (2 * g + 1) * LANES:(2 * g + 2) * LANES] = out[WINDOW:].astype(BF16)

    c0 = ATTN_WIDTH + 2 * KV_WIDTH
    u = proj_ref[:, c0:c0 + CONV_WIDTH]
    b_gate = proj_ref[:, c0 + CONV_WIDTH:c0 + 2 * CONV_WIDTH]
    c_gate = proj_ref[:, c0 + 2 * CONV_WIDTH:c0 + 3 * CONV_WIDTH]
    z_ref[zpad:zpad + ts, :] = c_gate * u
    y = cw_ref[0:1, :] * z_ref[zpad - 2:zpad - 2 + ts, :]
    y = y + cw_ref[1:2, :] * z_ref[zpad - 1:zpad - 1 + ts, :]
    y = y + cw_ref[2:3, :] * z_ref[zpad:zpad + ts, :]
    mix_ref[:, ATTN_WIDTH:] = (b_gate * y).astype(BF16)

    kvar_ref[:, 0:WINDOW, :] = kvar_ref[:, ts:ts + WINDOW, :]
    vvar_ref[:, 0:WINDOW, :] = vvar_ref[:, ts:ts + WINDOW, :]
    z_ref[0:zpad, :] = z_ref[ts:ts + zpad, :]

    mix = jnp.dot(mix_ref[...], wout_ref[...], preferred_element_type=F32)
    o_ref[0] = _layer_norm(alpha * x + (1.0 + gate) * mix, g_ref[...], b_ref[...])


def _mixer_block(x, mod, positions, w_in, conv_w, sinks, w_out, ln_g, ln_b, *, alpha):
    bsz, seq, d = x.shape
    ts = MIX_ROWS
    inv_freq = jnp.power(jnp.float32(ROPE_THETA),
                         -jnp.arange(0, ROT_DIM, 2, dtype=F32) / ROT_DIM)
    m = jnp.arange(LANES) % HEAD_DIM
    invf = jnp.where(m < ROT_DIM, inv_freq[m % (ROT_DIM // 2)], 0.0).reshape(1, LANES)
    const2 = lambda b, s, sink: (0, 0)
    grid_spec = pltpu.PrefetchScalarGridSpec(
        num_scalar_prefetch=1,
        grid=(bsz, seq // ts),
        in_specs=[
            pl.BlockSpec((1, ts, d), lambda b, s, sink: (b, s, 0)),
            pl.BlockSpec((1, N_MOD, d), lambda b, s, sink: (b, 0, 0)),
            pl.BlockSpec((1, ts, 1), lambda b, s, sink: (b, s, 0)),
            pl.BlockSpec((1, LANES), const2),
            pl.BlockSpec((d, IN_WIDTH), const2, pipeline_mode=pl.Buffered(1)),
            pl.BlockSpec((CONV_W, CONV_WIDTH), const2),
            pl.BlockSpec((d, d), const2, pipeline_mode=pl.Buffered(1)),
            pl.BlockSpec((1, d), const2),
            pl.BlockSpec((1, d), const2),
        ],
        out_specs=pl.BlockSpec((1, ts, d), lambda b, s, sink: (b, s, 0)),
        scratch_shapes=[
            pltpu.VMEM((ts, d), BF16),
            pltpu.VMEM((ts, IN_WIDTH), F32),
            pltpu.VMEM((ts, ATTN_WIDTH), BF16),
            pltpu.VMEM((4, ts + WINDOW, KV_WIDTH), BF16),
            pltpu.VMEM((4, ts + WINDOW, KV_WIDTH), BF16),
            pltpu.VMEM((ts + SUBLANES, CONV_WIDTH), F32),
            pltpu.VMEM((ts, d), BF16),
        ],
    )
    return pl.pallas_call(
        functools.partial(_mixer_kernel, alpha=alpha),
        grid_spec=grid_spec,
        out_shape=jax.ShapeDtypeStruct((bsz, seq, d), F32),
        compiler_params=pltpu.CompilerParams(
            dimension_semantics=("arbitrary", "arbitrary"),
            vmem_limit_bytes=VMEM_LIMIT_BYTES),
        name="mixer_block",
    )(sinks, x, mod, positions.reshape(bsz, seq, 1), invf, w_in.astype(BF16), conv_w,
      w_out.astype(BF16), ln_g.reshape(1, d), ln_b.reshape(1, d))


def kernel(x, c, positions, w_ada, b_ada, ffn1_w_gate_up, ffn1_w_down, ln1_g, ln1_b, w_in, conv_w, attn_sinks, w_out, ln2_g, ln2_b, ffn2_w_gate_up, ffn2_w_down, ln3_g, ln3_b):
    depth = w_ada.shape[0]
    alpha = (2.0 * depth) ** 0.25
    bsz = x.shape[0]
    for l in range(depth):
        mod = _ada_mod(c, w_ada[l], b_ada[l]).reshape(bsz, N_MOD, D_MODEL)
        x = _ffn_block(x, mod, ffn1_w_gate_up[l], ffn1_w_down[l], ln1_g[l], ln1_b[l],
                       mod_base=0, alpha=alpha)
        x = _mixer_block(x, mod, positions, w_in[l], conv_w[l], attn_sinks[l], w_out[l],
                         ln2_g[l], ln2_b[l], alpha=alpha)
        x = _ffn_block(x, mod, ffn2_w_gate_up[l], ffn2_w_down[l], ln3_g[l], ln3_b[l],
                       mod_base=6, alpha=alpha)
    return x
```

```python
import functools

import jax
import jax.numpy as jnp
from jax import lax
from jax.experimental import pallas as pl
from jax.experimental.pallas import tpu as pltpu

D_MODEL = 1024
HEAD_DIM = 64
ATTN_WIDTH = 512
CONV_WIDTH = 512
N_Q_HEADS = 8
N_KV_HEADS = 2
KV_WIDTH = N_KV_HEADS * HEAD_DIM
WINDOW = 128
ROT_DIM = 16
ROPE_THETA = 500000.0
CONV_W = 3
D_FF = 2816
N_MOD = 9
LN_EPS = 1e-5
IN_WIDTH = ATTN_WIDTH + 2 * KV_WIDTH + 3 * CONV_WIDTH

LANES = 128
SUBLANES = 8
VMEM_LIMIT_BYTES = 56 * 1024 * 1024

FFN_ROWS = 1024
FFN_SUB_ROWS = 512
FFN_CHUNK = 256
MIX_ROWS = 1024
MIX_SUB_ROWS = 512
MIX_SUB_BUFFERS = 6
MIX_CONV_CHUNK = 256
NEG_BIG = -1e30

F32 = jnp.float32
BF16 = jnp.bfloat16


def _layer_norm(y, g, b):
    mu = jnp.mean(y, axis=-1, keepdims=True)
    d = y - mu
    var = jnp.mean(d * d, axis=-1, keepdims=True)
    return d * lax.rsqrt(var + LN_EPS) * g + b


def _silu(x):
    return x * jax.nn.sigmoid(x)


def _ada_kernel(c_ref, w_ref, b_ref, o_ref):
    cond = _silu(c_ref[...]).astype(BF16)
    o_ref[...] = jnp.dot(cond, w_ref[...].astype(BF16),
                         preferred_element_type=F32) + b_ref[...]


def _ada_mod(c, w_ada, b_ada):
    bsz, d = c.shape
    n = w_ada.shape[1]
    tn = 9 * LANES
    return pl.pallas_call(
        _ada_kernel,
        grid=(n // tn,),
        in_specs=[
            pl.BlockSpec((bsz, d), lambda j: (0, 0)),
            pl.BlockSpec((d, tn), lambda j: (0, j)),
            pl.BlockSpec((1, tn), lambda j: (0, j)),
        ],
        out_specs=pl.BlockSpec((bsz, tn), lambda j: (0, j)),
        out_shape=jax.ShapeDtypeStruct((bsz, n), F32),
        compiler_params=pltpu.CompilerParams(dimension_semantics=("arbitrary",)),
        name="ada_mod",
    )(c, w_ada, b_ada.reshape(1, n))


def _ffn_kernel(x_ref, mod_ref, wgu_ref, wd_ref, g_ref, b_ref, o_ref, *scratch,
                mod_base, alpha):
    d_ff = wd_ref.shape[0]
    tf = FFN_CHUNK
    n_sub = len(scratch) // 2
    sh = mod_ref[0, mod_base:mod_base + 1, :]
    sc = mod_ref[0, mod_base + 1:mod_base + 2, :]
    gate = mod_ref[0, mod_base + 2:mod_base + 3, :]
    for r in range(n_sub):
        h_ref, acc_ref = scratch[2 * r], scratch[2 * r + 1]
        sub = h_ref.shape[0]
        rows = slice(r * sub, (r + 1) * sub)
        x = x_ref[0, rows, :]
        h_ref[...] = (x * (1.0 + sc) + sh).astype(BF16)
        for c in range(d_ff // tf):
            lo = c * tf
            g = jnp.dot(h_ref[...], wgu_ref[:, lo:lo + tf], preferred_element_type=F32)
            u = jnp.dot(h_ref[...], wgu_ref[:, d_ff + lo:d_ff + lo + tf],
                        preferred_element_type=F32)
            act = (_silu(g) * u).astype(BF16)
            d = jnp.dot(act, wd_ref[lo:lo + tf, :], preferred_element_type=F32)
            if c == 0:
                acc_ref[...] = d
            else:
                acc_ref[...] += d
        y = alpha * x + (0.5 * (1.0 + gate)) * acc_ref[...]
        o_ref[0, rows, :] = _layer_norm(y, g_ref[...], b_ref[...])


def _ffn_block(x, mod, w_gate_up, w_down, ln_g, ln_b, *, mod_base, alpha):
    bsz, seq, d = x.shape
    tm = FFN_ROWS
    const2 = lambda b, s: (0, 0)
    return pl.pallas_call(
        functools.partial(_ffn_kernel, mod_base=mod_base, alpha=alpha),
        grid=(bsz, seq // tm),
        in_specs=[
            pl.BlockSpec((1, tm, d), lambda b, s: (b, s, 0)),
            pl.BlockSpec((1, N_MOD, d), lambda b, s: (b, 0, 0)),
            pl.BlockSpec((d, 2 * D_FF), const2, pipeline_mode=pl.Buffered(1)),
            pl.BlockSpec((D_FF, d), const2, pipeline_mode=pl.Buffered(1)),
            pl.BlockSpec((1, d), const2),
            pl.BlockSpec((1, d), const2),
        ],
        out_specs=pl.BlockSpec((1, tm, d), lambda b, s: (b, s, 0)),
        out_shape=jax.ShapeDtypeStruct((bsz, seq, d), F32),
        scratch_shapes=[pltpu.VMEM((FFN_SUB_ROWS, d), dt)
                        for _ in range(tm // FFN_SUB_ROWS) for dt in (BF16, F32)],
        compiler_params=pltpu.CompilerParams(
            dimension_semantics=("arbitrary", "arbitrary"),
            vmem_limit_bytes=VMEM_LIMIT_BYTES),
        name="ffn_block",
    )(x, mod, w_gate_up.astype(BF16), w_down.astype(BF16), ln_g.reshape(1, d), ln_b.reshape(1, d))


def _mixer_kernel(x_ref, mod_ref, pos_ref, invf_ref, sink_ref, wqkv_ref, wconv_ref, cw_ref,
                  wout_ref, g_ref, b_ref, o_ref,
                  k_ref, vt_ref, z_ref, bias_ref, *sub_scratch, alpha):
    ts = x_ref.shape[1]
    n_sub = len(sub_scratch) // MIX_SUB_BUFFERS
    sub = ts // n_sub
    s_idx = pl.program_id(1)
    zpad = SUBLANES
    n_heads_lanes = N_Q_HEADS * WINDOW
    n_slabs = CONV_WIDTH // LANES

    @pl.when(s_idx == 0)
    def _():
        k_ref[0:WINDOW, :] = jnp.zeros((WINDOW, KV_WIDTH), BF16)
        vt_ref[:, 0:WINDOW] = jnp.zeros((KV_WIDTH, WINDOW), BF16)
        z_ref[:, 0:zpad, :] = jnp.zeros((n_slabs, zpad, LANES), F32)

    sh = mod_ref[0, 3:4, :]
    sc = mod_ref[0, 4:5, :]
    gate = mod_ref[0, 5:6, :]
    contract_last = (((1,), (1,)), ((), ()))
    contract_first = (((0,), (0,)), ((), ()))
    half = ROT_DIM // 2
    scale = HEAD_DIM ** -0.5

    rows2 = 2 * WINDOW
    ki = lax.broadcasted_iota(jnp.int32, (rows2, n_heads_lanes), 0)
    qi = lax.broadcasted_iota(jnp.int32, (rows2, n_heads_lanes), 1) % WINDOW
    bias_ref[...] = jnp.where((ki > qi) & (ki <= qi + WINDOW), 0.0, NEG_BIG)
    no_prev = jnp.where(s_idx == 0, WINDOW, 0)
    first_pen = jnp.where(lax.broadcasted_iota(jnp.int32, (rows2, 1), 0) < no_prev, NEG_BIG, 0.0)
    sink = sink_ref[...]

    for r in range(n_sub):
        h_ref, pt_ref, qbd_ref, at_ref, pc_ref, cv_ref = sub_scratch[
            r * MIX_SUB_BUFFERS:(r + 1) * MIX_SUB_BUFFERS]
        t0 = r * sub
        x = x_ref[0, t0:t0 + sub, :]
        h_ref[...] = (x * (1.0 + sc) + sh).astype(BF16)
        pt_ref[...] = lax.dot_general(wqkv_ref[...], h_ref[...], contract_last,
                                      preferred_element_type=F32)

        ang = invf_ref[...] * pos_ref[0, :, t0:t0 + sub].astype(F32)
        cos_t = jnp.cos(ang)
        sin_t = jnp.sin(ang)

        def rope(head):
            x1 = head[0:half]
            x2 = head[half:ROT_DIM]
            return jnp.concatenate(
                [x1 * cos_t - x2 * sin_t, x2 * cos_t + x1 * sin_t, head[ROT_DIM:]], axis=0)

        qbd_ref[...] = jnp.zeros(qbd_ref.shape, BF16)
        for hd in range(N_Q_HEADS):
            kv = hd // (N_Q_HEADS // N_KV_HEADS)
            qh = (rope(pt_ref[hd * HEAD_DIM:(hd + 1) * HEAD_DIM, :]) * scale).astype(BF16)
            for n in range(sub // WINDOW):
                qbd_ref[n, kv * HEAD_DIM:(kv + 1) * HEAD_DIM, hd * WINDOW:(hd + 1) * WINDOW] = (
                    qh[:, n * WINDOW:(n + 1) * WINDOW])
        kt = jnp.concatenate(
            [rope(pt_ref[ATTN_WIDTH + kv * HEAD_DIM:ATTN_WIDTH + (kv + 1) * HEAD_DIM, :])
             for kv in range(N_KV_HEADS)], axis=0)
        k_ref[WINDOW + t0:WINDOW + t0 + sub, :] = kt.T.astype(BF16)
        vt_ref[:, WINDOW + t0:WINDOW + t0 + sub] = pt_ref[
            ATTN_WIDTH + KV_WIDTH:ATTN_WIDTH + 2 * KV_WIDTH, :].astype(BF16)

        def conv_proj(c, h_ref=h_ref, pc_ref=pc_ref):
            cols = slice(c * MIX_CONV_CHUNK, (c + 1) * MIX_CONV_CHUNK)
            pc_ref[:, cols] = jnp.dot(h_ref[...], wconv_ref[:, cols], preferred_element_type=F32)

        def scores(n, qbd_ref=qbd_ref, t0=t0):
            r0 = t0 + n * WINDOW
            return jnp.dot(k_ref[r0:r0 + rows2, :], qbd_ref[n], preferred_element_type=F32)

        def attend(n, sco, at_ref=at_ref, t0=t0):
            r0 = t0 + n * WINDOW
            sco = sco + bias_ref[...]
            if r0 == 0:
                sco = sco + first_pen
            mx = jnp.maximum(jnp.max(sco, axis=0, keepdims=True), sink)
            pr = jnp.exp(sco - mx)
            den = jnp.sum(pr, axis=0, keepdims=True) + jnp.exp(sink - mx)
            pv = jnp.dot(vt_ref[:, r0:r0 + rows2], pr.astype(BF16), preferred_element_type=F32)
            pv = pv * (1.0 / den)
            for hd in range(N_Q_HEADS):
                kv = hd // (N_Q_HEADS // N_KV_HEADS)
                at_ref[hd * HEAD_DIM:(hd + 1) * HEAD_DIM, n * WINDOW:(n + 1) * WINDOW] = (
                    pv[kv * HEAD_DIM:(kv + 1) * HEAD_DIM,
                       hd * WINDOW:(hd + 1) * WINDOW].astype(BF16))

        n_blocks = sub // WINDOW
        n_chunks = 3 * CONV_WIDTH // MIX_CONV_CHUNK
        pending = scores(0)
        chunk = 0
        for n in range(n_blocks):
            conv_proj(chunk)
            chunk += 1
            upcoming = scores(n + 1) if n + 1 < n_blocks else None
            attend(n, pending)
            pending = upcoming
        while chunk < n_chunks:
            conv_proj(chunk)
            chunk += 1
        mix = lax.dot_general(at_ref[...], wout_ref[0:ATTN_WIDTH, :], contract_first,
                              preferred_element_type=F32)

        z0 = zpad + t0
        for j in range(n_slabs):
            lanes = slice(j * LANES, (j + 1) * LANES)
            u = pc_ref[:, j * LANES:(j + 1) * LANES]
            b_gate = pc_ref[:, CONV_WIDTH + j * LANES:CONV_WIDTH + (j + 1) * LANES]
            c_gate = pc_ref[:, 2 * CONV_WIDTH + j * LANES:2 * CONV_WIDTH + (j + 1) * LANES]
            z_ref[j, z0:z0 + sub, :] = c_gate * u
            y = cw_ref[0:1, lanes] * z_ref[j, z0 - 2:z0 - 2 + sub, :]
            y = y + cw_ref[1:2, lanes] * z_ref[j, z0 - 1:z0 - 1 + sub, :]
            y = y + cw_ref[2:3, lanes] * z_ref[j, z0:z0 + sub, :]
            cv_ref[:, lanes] = (b_gate * y).astype(BF16)

        mix = mix + jnp.dot(cv_ref[...], wout_ref[ATTN_WIDTH:, :], preferred_element_type=F32)
        o_ref[0, t0:t0 + sub, :] = _layer_norm(alpha * x + (1.0 + gate) * mix,
                                               g_ref[...], b_ref[...])

    k_ref[0:WINDOW, :] = k_ref[ts:ts + WINDOW, :]
    vt_ref[:, 0:WINDOW] = vt_ref[:, ts:ts + WINDOW]
    z_ref[:, 0:zpad, :] = z_ref[:, ts:ts + zpad, :]


def _mixer_block(x, mod, positions, w_in, conv_w, sinks, w_out, ln_g, ln_b, *, alpha):
    bsz, seq, d = x.shape
    ts = MIX_ROWS
    sub = MIX_SUB_ROWS
    qkv = ATTN_WIDTH + 2 * KV_WIDTH
    inv_freq = jnp.power(jnp.float32(ROPE_THETA),
                         -jnp.arange(0, ROT_DIM, 2, dtype=F32) / ROT_DIM)
    sink_row = jnp.repeat(sinks.astype(F32), WINDOW).reshape(1, N_Q_HEADS * WINDOW)
    const2 = lambda b, s: (0, 0)
    sub_scratch = [
        pltpu.VMEM((sub, d), BF16),
        pltpu.VMEM((qkv, sub), F32),
        pltpu.VMEM((sub // WINDOW, KV_WIDTH, N_Q_HEADS * WINDOW), BF16),
        pltpu.VMEM((ATTN_WIDTH, sub), BF16),
        pltpu.VMEM((sub, 3 * CONV_WIDTH), F32),
        pltpu.VMEM((sub, CONV_WIDTH), BF16),
    ]
    assert len(sub_scratch) == MIX_SUB_BUFFERS
    return pl.pallas_call(
        functools.partial(_mixer_kernel, alpha=alpha),
        grid=(bsz, seq // ts),
        in_specs=[
            pl.BlockSpec((1, ts, d), lambda b, s: (b, s, 0)),
            pl.BlockSpec((1, N_MOD, d), lambda b, s: (b, 0, 0)),
            pl.BlockSpec((1, 1, ts), lambda b, s: (b, 0, s)),
            pl.BlockSpec((ROT_DIM // 2, 1), const2),
            pl.BlockSpec((1, N_Q_HEADS * WINDOW), const2),
            pl.BlockSpec((qkv, d), const2, pipeline_mode=pl.Buffered(1)),
            pl.BlockSpec((d, 3 * CONV_WIDTH), const2, pipeline_mode=pl.Buffered(1)),
            pl.BlockSpec((CONV_W, CONV_WIDTH), const2),
            pl.BlockSpec((d, d), const2, pipeline_mode=pl.Buffered(1)),
            pl.BlockSpec((1, d), const2),
            pl.BlockSpec((1, d), const2),
        ],
        out_specs=pl.BlockSpec((1, ts, d), lambda b, s: (b, s, 0)),
        out_shape=jax.ShapeDtypeStruct((bsz, seq, d), F32),
        scratch_shapes=[
            pltpu.VMEM((ts + WINDOW, KV_WIDTH), BF16),
            pltpu.VMEM((KV_WIDTH, ts + WINDOW), BF16),
            pltpu.VMEM((CONV_WIDTH // LANES, ts + SUBLANES, LANES), F32),
            pltpu.VMEM((2 * WINDOW, N_Q_HEADS * WINDOW), F32),
        ] + sub_scratch * (ts // sub),
        compiler_params=pltpu.CompilerParams(
            dimension_semantics=("arbitrary", "arbitrary"),
            vmem_limit_bytes=VMEM_LIMIT_BYTES),
        name="mixer_block",
    )(x, mod, positions.reshape(bsz, 1, seq), inv_freq.reshape(ROT_DIM // 2, 1), sink_row,
      w_in[:, :qkv].T.astype(BF16), w_in[:, qkv:].astype(BF16), conv_w,
      w_out.astype(BF16), ln_g.reshape(1, d), ln_b.reshape(1, d))


def kernel(x, c, positions, w_ada, b_ada, ffn1_w_gate_up, ffn1_w_down, ln1_g, ln1_b, w_in, conv_w, attn_sinks, w_out, ln2_g, ln2_b, ffn2_w_gate_up, ffn2_w_down, ln3_g, ln3_b):
    depth = w_ada.shape[0]
    alpha = (2.0 * depth) ** 0.25
    bsz = x.shape[0]
    for l in range(depth):
        mod = _ada_mod(c, w_ada[l], b_ada[l]).reshape(bsz, N_MOD, D_MODEL)
        x = _ffn_block(x, mod, ffn1_w_gate_up[l], ffn1_w_down[l], ln1_g[l], ln1_b[l],
                       mod_base=0, alpha=alpha)
        x = _mixer_block(x, mod, positions, w_in[l], conv_w[l], attn_sinks[l], w_out[l],
                         ln2_g[l], ln2_b[l], alpha=alpha)
        x = _ffn_block(x, mod, ffn2_w_gate_up[l], ffn2_w_down[l], ln3_g[l], ln3_b[l],
                       mod_base=6, alpha=alpha)
    return x
```

```python
import functools

import jax
import jax.numpy as jnp
from jax import lax
from jax.experimental import pallas as pl
from jax.experimental.pallas import tpu as pltpu

D_MODEL = 1024
HEAD_DIM = 64
ATTN_WIDTH = 512
CONV_WIDTH = 512
N_Q_HEADS = 8
N_KV_HEADS = 2
KV_WIDTH = N_KV_HEADS * HEAD_DIM
WINDOW = 128
ROT_DIM = 16
ROPE_THETA = 500000.0
CONV_W = 3
D_FF = 2816
N_MOD = 9
LN_EPS = 1e-5
IN_WIDTH = ATTN_WIDTH + 2 * KV_WIDTH + 3 * CONV_WIDTH

LANES = 128
SUBLANES = 8
BF16_SUBLANES = 16
VMEM_LIMIT_BYTES = 56 * 1024 * 1024

FFN_ROWS = 1024
FFN_SUB_ROWS = 512
FFN_CHUNK = 256
MIX_ROWS = 1024
MIX_SUB_ROWS = 512
MIX_SUB_BUFFERS = 6
MIX_CONV_CHUNK = 256
NEG_BIG = -1e30
LOG2_E = 1.4426950408889634

F32 = jnp.float32
BF16 = jnp.bfloat16


def _layer_norm(y, g, b):
    mu = jnp.mean(y, axis=-1, keepdims=True)
    d = y - mu
    var = jnp.mean(d * d, axis=-1, keepdims=True)
    return d * lax.rsqrt(var + LN_EPS) * g + b


def _silu(x):
    return x * jax.nn.sigmoid(x)


def _ada_kernel(c_ref, w_ref, b_ref, o_ref):
    cond = _silu(c_ref[...]).astype(BF16)
    o_ref[...] = jnp.dot(cond, w_ref[...].astype(BF16),
                         preferred_element_type=F32) + b_ref[...]


def _ada_mod(c, w_ada, b_ada):
    bsz, d = c.shape
    n = w_ada.shape[1]
    tn = 9 * LANES
    return pl.pallas_call(
        _ada_kernel,
        grid=(n // tn,),
        in_specs=[
            pl.BlockSpec((bsz, d), lambda j: (0, 0)),
            pl.BlockSpec((d, tn), lambda j: (0, j)),
            pl.BlockSpec((1, tn), lambda j: (0, j)),
        ],
        out_specs=pl.BlockSpec((bsz, tn), lambda j: (0, j)),
        out_shape=jax.ShapeDtypeStruct((bsz, n), F32),
        compiler_params=pltpu.CompilerParams(dimension_semantics=("arbitrary",)),
        name="ada_mod",
    )(c, w_ada, b_ada.reshape(1, n))


def _ffn_kernel(*refs, n_cast, mod_base, alpha):
    x_ref, mod_ref, wgu_ref, wd_ref, g_ref, b_ref = refs[:6]
    cast_in = refs[6:6 + n_cast]
    o_ref = refs[6 + n_cast]
    cast_out = refs[7 + n_cast:7 + 2 * n_cast]
    scratch = refs[7 + 2 * n_cast:]
    for src_ref, dst_ref in zip(cast_in, cast_out):
        dst_ref[...] = src_ref[...].astype(BF16)

    d_ff = wd_ref.shape[0]
    tf = FFN_CHUNK
    n_sub = len(scratch) // 2
    sh = mod_ref[0, mod_base:mod_base + 1, :]
    sc = mod_ref[0, mod_base + 1:mod_base + 2, :]
    gate = mod_ref[0, mod_base + 2:mod_base + 3, :]
    for r in range(n_sub):
        h_ref, acc_ref = scratch[2 * r], scratch[2 * r + 1]
        sub = h_ref.shape[0]
        rows = slice(r * sub, (r + 1) * sub)
        x = x_ref[0, rows, :]
        h_ref[...] = (x * (1.0 + sc) + sh).astype(BF16)
        for c in range(d_ff // tf):
            lo = c * tf
            g = jnp.dot(h_ref[...], wgu_ref[:, lo:lo + tf], preferred_element_type=F32)
            u = jnp.dot(h_ref[...], wgu_ref[:, d_ff + lo:d_ff + lo + tf],
                        preferred_element_type=F32)
            act = (_silu(g) * u).astype(BF16)
            d = jnp.dot(act, wd_ref[lo:lo + tf, :], preferred_element_type=F32)
            if c == 0:
                acc_ref[...] = d
            else:
                acc_ref[...] += d
        y = alpha * x + (0.5 * (1.0 + gate)) * acc_ref[...]
        o_ref[0, rows, :] = _layer_norm(y, g_ref[...], b_ref[...])


def _cast_chunk_rows(rows, n_steps):
    step = BF16_SUBLANES
    chunk = step
    while rows % chunk or rows // chunk > n_steps:
        chunk += step
    return chunk


def _ffn_block(x, mod, w_gate_up, w_down, ln_g, ln_b, *, mod_base, alpha, cast_jobs=()):
    bsz, seq, d = x.shape
    tm = FFN_ROWS
    per_seq = seq // tm
    n_steps = bsz * per_seq
    const2 = lambda b, s: (0, 0)
    cast_specs, cast_shapes = [], []
    for w in cast_jobs:
        rows, cols = w.shape
        chunk = _cast_chunk_rows(rows, n_steps)
        last = rows // chunk - 1
        cast_specs.append(pl.BlockSpec(
            (chunk, cols), lambda b, s, last=last: (jnp.minimum(b * per_seq + s, last), 0)))
        cast_shapes.append(jax.ShapeDtypeStruct((rows, cols), BF16))
    outs = pl.pallas_call(
        functools.partial(_ffn_kernel, n_cast=len(cast_jobs), mod_base=mod_base, alpha=alpha),
        grid=(bsz, per_seq),
        in_specs=[
            pl.BlockSpec((1, tm, d), lambda b, s: (b, s, 0)),
            pl.BlockSpec((1, N_MOD, d), lambda b, s: (b, 0, 0)),
            pl.BlockSpec((d, 2 * D_FF), const2, pipeline_mode=pl.Buffered(1)),
            pl.BlockSpec((D_FF, d), const2, pipeline_mode=pl.Buffered(1)),
            pl.BlockSpec((1, d), const2),
            pl.BlockSpec((1, d), const2),
        ] + cast_specs,
        out_specs=[pl.BlockSpec((1, tm, d), lambda b, s: (b, s, 0))] + cast_specs,
        out_shape=[jax.ShapeDtypeStruct((bsz, seq, d), F32)] + cast_shapes,
        scratch_shapes=[pltpu.VMEM((FFN_SUB_ROWS, d), dt)
                        for _ in range(tm // FFN_SUB_ROWS) for dt in (BF16, F32)],
        compiler_params=pltpu.CompilerParams(
            dimension_semantics=("arbitrary", "arbitrary"),
            vmem_limit_bytes=VMEM_LIMIT_BYTES),
        name="ffn_block",
    )(x, mod, w_gate_up, w_down, ln_g.reshape(1, d), ln_b.reshape(1, d), *cast_jobs)
    return outs[0], tuple(outs[1:])


def _mixer_kernel(x_ref, mod_ref, pos_ref, invf_ref, sink_ref, wqkv_ref, wconv_ref, cw_ref,
                  wout_ref, g_ref, b_ref, o_ref,
                  k_ref, vt_ref, z_ref, bias_ref, *sub_scratch, alpha):
    ts = x_ref.shape[1]
    n_sub = len(sub_scratch) // MIX_SUB_BUFFERS
    sub = ts // n_sub
    s_idx = pl.program_id(1)
    zpad = SUBLANES
    n_heads_lanes = N_Q_HEADS * WINDOW
    n_slabs = CONV_WIDTH // LANES

    @pl.when(s_idx == 0)
    def _():
        k_ref[0:WINDOW, :] = jnp.zeros((WINDOW, KV_WIDTH), BF16)
        vt_ref[:, 0:WINDOW] = jnp.zeros((KV_WIDTH, WINDOW), BF16)
        z_ref[:, 0:zpad, :] = jnp.zeros((n_slabs, zpad, LANES), F32)

    sh = mod_ref[0, 3:4, :]
    sc = mod_ref[0, 4:5, :]
    gate = mod_ref[0, 5:6, :]
    contract_last = (((1,), (1,)), ((), ()))
    contract_first = (((0,), (0,)), ((), ()))
    half = ROT_DIM // 2
    scale = HEAD_DIM ** -0.5 * LOG2_E

    rows2 = 2 * WINDOW
    ki = lax.broadcasted_iota(jnp.int32, (rows2, n_heads_lanes), 0)
    qi = lax.broadcasted_iota(jnp.int32, (rows2, n_heads_lanes), 1) % WINDOW
    bias_ref[...] = jnp.where((ki > qi) & (ki <= qi + WINDOW), 0.0, NEG_BIG)
    no_prev = jnp.where(s_idx == 0, WINDOW, 0)
    first_pen = jnp.where(lax.broadcasted_iota(jnp.int32, (rows2, 1), 0) < no_prev, NEG_BIG, 0.0)
    sink = sink_ref[...] * LOG2_E

    for r in range(n_sub):
        h_ref, pt_ref, qbd_ref, at_ref, pc_ref, cv_ref = sub_scratch[
            r * MIX_SUB_BUFFERS:(r + 1) * MIX_SUB_BUFFERS]
        t0 = r * sub
        x = x_ref[0, t0:t0 + sub, :]
        h_ref[...] = (x * (1.0 + sc) + sh).astype(BF16)
        pt_ref[...] = lax.dot_general(wqkv_ref[...], h_ref[...], contract_last,
                                      preferred_element_type=F32)

        ang = invf_ref[...] * pos_ref[0, :, t0:t0 + sub].astype(F32)
        cos_t = jnp.cos(ang)
        sin_t = jnp.sin(ang)

        def rope(head):
            x1 = head[0:half]
            x2 = head[half:ROT_DIM]
            return jnp.concatenate(
                [x1 * cos_t - x2 * sin_t, x2 * cos_t + x1 * sin_t, head[ROT_DIM:]], axis=0)

        qbd_ref[...] = jnp.zeros(qbd_ref.shape, BF16)
        for hd in range(N_Q_HEADS):
            kv = hd // (N_Q_HEADS // N_KV_HEADS)
            qh = (rope(pt_ref[hd * HEAD_DIM:(hd + 1) * HEAD_DIM, :]) * scale).astype(BF16)
            for n in range(sub // WINDOW):
                qbd_ref[n, kv * HEAD_DIM:(kv + 1) * HEAD_DIM, hd * WINDOW:(hd + 1) * WINDOW] = (
                    qh[:, n * WINDOW:(n + 1) * WINDOW])
        kt = jnp.concatenate(
            [rope(pt_ref[ATTN_WIDTH + kv * HEAD_DIM:ATTN_WIDTH + (kv + 1) * HEAD_DIM, :])
             for kv in range(N_KV_HEADS)], axis=0)
        k_ref[WINDOW + t0:WINDOW + t0 + sub, :] = kt.T.astype(BF16)
        vt_ref[:, WINDOW + t0:WINDOW + t0 + sub] = pt_ref[
            ATTN_WIDTH + KV_WIDTH:ATTN_WIDTH + 2 * KV_WIDTH, :].astype(BF16)

        def conv_proj(c, h_ref=h_ref, pc_ref=pc_ref):
            cols = slice(c * MIX_CONV_CHUNK, (c + 1) * MIX_CONV_CHUNK)
            pc_ref[:, cols] = jnp.dot(h_ref[...], wconv_ref[:, cols], preferred_element_type=F32)

        def scores(n, qbd_ref=qbd_ref, t0=t0):
            r0 = t0 + n * WINDOW
            return jnp.dot(k_ref[r0:r0 + rows2, :], qbd_ref[n], preferred_element_type=F32)

        def attend(n, sco, at_ref=at_ref, t0=t0):
            r0 = t0 + n * WINDOW
            sco = sco + bias_ref[...]
            if r0 == 0:
                sco = sco + first_pen
            mx = jnp.maximum(jnp.max(sco, axis=0, keepdims=True), sink)
            pr = jnp.exp2(sco - mx)
            den = jnp.sum(pr, axis=0, keepdims=True) + jnp.exp2(sink - mx)
            pv = jnp.dot(vt_ref[:, r0:r0 + rows2], pr.astype(BF16), preferred_element_type=F32)
            pv = pv * (1.0 / den)
            for hd in range(N_Q_HEADS):
                kv = hd // (N_Q_HEADS // N_KV_HEADS)
                at_ref[hd * HEAD_DIM:(hd + 1) * HEAD_DIM, n * WINDOW:(n + 1) * WINDOW] = (
                    pv[kv * HEAD_DIM:(kv + 1) * HEAD_DIM,
                       hd * WINDOW:(hd + 1) * WINDOW].astype(BF16))

        n_blocks = sub // WINDOW
        n_chunks = 3 * CONV_WIDTH // MIX_CONV_CHUNK
        pending = scores(0)
        chunk = 0
        for n in range(n_blocks):
            conv_proj(chunk)
            chunk += 1
            upcoming = scores(n + 1) if n + 1 < n_blocks else None
            attend(n, pending)
            pending = upcoming
        while chunk < n_chunks:
            conv_proj(chunk)
            chunk += 1
        mix = lax.dot_general(at_ref[...], wout_ref[0:ATTN_WIDTH, :], contract_first,
                              preferred_element_type=F32)

        z0 = zpad + t0
        for j in range(n_slabs):
            lanes = slice(j * LANES, (j + 1) * LANES)
            u = pc_ref[:, j * LANES:(j + 1) * LANES]
            b_gate = pc_ref[:, CONV_WIDTH + j * LANES:CONV_WIDTH + (j + 1) * LANES]
            c_gate = pc_ref[:, 2 * CONV_WIDTH + j * LANES:2 * CONV_WIDTH + (j + 1) * LANES]
            z_ref[j, z0:z0 + sub, :] = c_gate * u
            y = cw_ref[0:1, lanes] * z_ref[j, z0 - 2:z0 - 2 + sub, :]
            y = y + cw_ref[1:2, lanes] * z_ref[j, z0 - 1:z0 - 1 + sub, :]
            y = y + cw_ref[2:3, lanes] * z_ref[j, z0:z0 + sub, :]
            cv_ref[:, lanes] = (b_gate * y).astype(BF16)

        mix = mix + jnp.dot(cv_ref[...], wout_ref[ATTN_WIDTH:, :], preferred_element_type=F32)
        o_ref[0, t0:t0 + sub, :] = _layer_norm(alpha * x + (1.0 + gate) * mix,
                                               g_ref[...], b_ref[...])

    k_ref[0:WINDOW, :] = k_ref[ts:ts + WINDOW, :]
    vt_ref[:, 0:WINDOW] = vt_ref[:, ts:ts + WINDOW]
    z_ref[:, 0:zpad, :] = z_ref[:, ts:ts + zpad, :]


def _mixer_block(x, mod, positions, w_in, conv_w, sinks, w_out, ln_g, ln_b, *, alpha):
    bsz, seq, d = x.shape
    ts = MIX_ROWS
    sub = MIX_SUB_ROWS
    qkv = ATTN_WIDTH + 2 * KV_WIDTH
    inv_freq = jnp.power(jnp.float32(ROPE_THETA),
                         -jnp.arange(0, ROT_DIM, 2, dtype=F32) / ROT_DIM)
    sink_row = jnp.repeat(sinks.astype(F32), WINDOW).reshape(1, N_Q_HEADS * WINDOW)
    const2 = lambda b, s: (0, 0)
    sub_scratch = [
        pltpu.VMEM((sub, d), BF16),
        pltpu.VMEM((qkv, sub), F32),
        pltpu.VMEM((sub // WINDOW, KV_WIDTH, N_Q_HEADS * WINDOW), BF16),
        pltpu.VMEM((ATTN_WIDTH, sub), BF16),
        pltpu.VMEM((sub, 3 * CONV_WIDTH), F32),
        pltpu.VMEM((sub, CONV_WIDTH), BF16),
    ]
    assert len(sub_scratch) == MIX_SUB_BUFFERS
    return pl.pallas_call(
        functools.partial(_mixer_kernel, alpha=alpha),
        grid=(bsz, seq // ts),
        in_specs=[
            pl.BlockSpec((1, ts, d), lambda b, s: (b, s, 0)),
            pl.BlockSpec((1, N_MOD, d), lambda b, s: (b, 0, 0)),
            pl.BlockSpec((1, 1, ts), lambda b, s: (b, 0, s)),
            pl.BlockSpec((ROT_DIM // 2, 1), const2),
            pl.BlockSpec((1, N_Q_HEADS * WINDOW), const2),
            pl.BlockSpec((qkv, d), const2, pipeline_mode=pl.Buffered(1)),
            pl.BlockSpec((d, 3 * CONV_WIDTH), const2, pipeline_mode=pl.Buffered(1)),
            pl.BlockSpec((CONV_W, CONV_WIDTH), const2),
            pl.BlockSpec((d, d), const2, pipeline_mode=pl.Buffered(1)),
            pl.BlockSpec((1, d), const2),
            pl.BlockSpec((1, d), const2),
        ],
        out_specs=pl.BlockSpec((1, ts, d), lambda b, s: (b, s, 0)),
        out_shape=jax.ShapeDtypeStruct((bsz, seq, d), F32),
        scratch_shapes=[
            pltpu.VMEM((ts + WINDOW, KV_WIDTH), BF16),
            pltpu.VMEM((KV_WIDTH, ts + WINDOW), BF16),
            pltpu.VMEM((CONV_WIDTH // LANES, ts + SUBLANES, LANES), F32),
            pltpu.VMEM((2 * WINDOW, N_Q_HEADS * WINDOW), F32),
        ] + sub_scratch * (ts // sub),
        compiler_params=pltpu.CompilerParams(
            dimension_semantics=("arbitrary", "arbitrary"),
            vmem_limit_bytes=VMEM_LIMIT_BYTES),
        name="mixer_block",
    )(x, mod, positions.reshape(bsz, 1, seq), inv_freq.reshape(ROT_DIM // 2, 1), sink_row,
      w_in[:, :qkv].T, w_in[:, qkv:], conv_w, w_out, ln_g.reshape(1, d), ln_b.reshape(1, d))


def kernel(x, c, positions, w_ada, b_ada, ffn1_w_gate_up, ffn1_w_down, ln1_g, ln1_b, w_in, conv_w, attn_sinks, w_out, ln2_g, ln2_b, ffn2_w_gate_up, ffn2_w_down, ln3_g, ln3_b):
    depth = w_ada.shape[0]
    alpha = (2.0 * depth) ** 0.25
    bsz = x.shape[0]
    ffn1_w = (ffn1_w_gate_up[0].astype(BF16), ffn1_w_down[0].astype(BF16))
    for l in range(depth):
        mod = _ada_mod(c, w_ada[l], b_ada[l]).reshape(bsz, N_MOD, D_MODEL)
        x, (w_in_b, w_out_b, ffn2_wgu, ffn2_wd) = _ffn_block(
            x, mod, *ffn1_w, ln1_g[l], ln1_b[l], mod_base=0, alpha=alpha,
            cast_jobs=(w_in[l], w_out[l], ffn2_w_gate_up[l], ffn2_w_down[l]))
        x = _mixer_block(x, mod, positions, w_in_b, conv_w[l], attn_sinks[l], w_out_b,
                         ln2_g[l], ln2_b[l], alpha=alpha)
        next_ffn1 = (ffn1_w_gate_up[l + 1], ffn1_w_down[l + 1]) if l + 1 < depth else ()
        x, ffn1_w = _ffn_block(x, mod, ffn2_wgu, ffn2_wd, ln3_g[l], ln3_b[l],
                               mod_base=6, alpha=alpha, cast_jobs=next_ffn1)
    return x
```

```python
import functools
from typing import NamedTuple

import jax
import jax.numpy as jnp
from jax import lax
from jax.experimental import pallas as pl
from jax.experimental.pallas import tpu as pltpu

D_MODEL = 1024
HEAD_DIM = 64
ATTN_WIDTH = 512
CONV_WIDTH = 512
N_Q_HEADS = 8
N_KV_HEADS = 2
KV_WIDTH = N_KV_HEADS * HEAD_DIM
WINDOW = 128
ROT_DIM = 16
ROPE_THETA = 500000.0
CONV_W = 3
D_FF = 2816
N_MOD = 9
LN_EPS = 1e-5
IN_WIDTH = ATTN_WIDTH + 2 * KV_WIDTH + 3 * CONV_WIDTH

LANES = 128
SUBLANES = 8
BF16_SUBLANES = 16
VMEM_LIMIT_BYTES = 56 * 1024 * 1024

FFN_ROWS = 1024
FFN_SUB_ROWS = (512, 512)
FFN_CHUNK = 256
LAST_PIECES = 2
MIX_ROWS = 1024
MIX_SUB_ROWS = 512
MIX_SUB_BUFFERS = 6
MIX_CONV_CHUNK = 256
NEG_BIG = -1e30
LOG2_E = 1.4426950408889634

F32 = jnp.float32
BF16 = jnp.bfloat16


def _layer_norm(y, g, b):
    mu = jnp.mean(y, axis=-1, keepdims=True)
    d = y - mu
    var = jnp.mean(d * d, axis=-1, keepdims=True)
    return d * lax.rsqrt(var + LN_EPS) * g + b


def _silu(x):
    return x * jax.nn.sigmoid(x)


class _CastJob(NamedTuple):
    src: jax.Array
    col0: int
    cols: int
    transpose: bool = False


def _whole(w):
    return _CastJob(w, 0, w.shape[1])


def _cast_plan(jobs, n_steps, flat_step):
    in_specs, out_specs, out_shapes = [], [], []
    for job in jobs:
        rows = job.src.shape[0]
        align = LANES if job.transpose else BF16_SUBLANES
        chunk = align
        while rows % chunk or rows // chunk > n_steps:
            chunk += align
        last = rows // chunk - 1
        col_block = job.col0 // job.cols
        assert col_block * job.cols == job.col0

        def chunk_of(*g, last=last):
            return jnp.minimum(flat_step(*g), last)

        in_specs.append(pl.BlockSpec(
            (chunk, job.cols), lambda *g, f=chunk_of, cb=col_block: (f(*g), cb)))
        if job.transpose:
            out_specs.append(pl.BlockSpec((job.cols, chunk), lambda *g, f=chunk_of: (0, f(*g))))
            out_shapes.append(jax.ShapeDtypeStruct((job.cols, rows), BF16))
        else:
            out_specs.append(pl.BlockSpec((chunk, job.cols), lambda *g, f=chunk_of: (f(*g), 0)))
            out_shapes.append(jax.ShapeDtypeStruct((rows, job.cols), BF16))
    return in_specs, out_specs, out_shapes


def _run_cast_jobs(src_refs, dst_refs, transposes):
    for src_ref, dst_ref, transpose in zip(src_refs, dst_refs, transposes):
        w = src_ref[...]
        dst_ref[...] = (w.T if transpose else w).astype(BF16)


def _ada_kernel(*refs, transposes):
    n_cast = len(transposes)
    c_ref, w_ref, b_ref = refs[:3]
    o_ref = refs[3 + n_cast]
    _run_cast_jobs(refs[3:3 + n_cast], refs[4 + n_cast:], transposes)
    cond = _silu(c_ref[...]).astype(BF16)
    o_ref[0] = jnp.dot(cond, w_ref[...].astype(BF16), preferred_element_type=F32) + b_ref[0]


def _ada_mod(c, w_ada, b_ada, cast_jobs=()):
    bsz, d = c.shape
    cast_in, cast_out, cast_shapes = _cast_plan(cast_jobs, N_MOD, lambda j: j)
    outs = pl.pallas_call(
        functools.partial(_ada_kernel, transposes=tuple(j.transpose for j in cast_jobs)),
        grid=(N_MOD,),
        in_specs=[
            pl.BlockSpec((bsz, d), lambda j: (0, 0)),
            pl.BlockSpec((d, d), lambda j: (0, j)),
            pl.BlockSpec((1, 1, d), lambda j: (j, 0, 0)),
        ] + cast_in,
        out_specs=[pl.BlockSpec((1, bsz, d), lambda j: (j, 0, 0))] + cast_out,
        out_shape=[jax.ShapeDtypeStruct((N_MOD, bsz, d), F32)] + cast_shapes,
        compiler_params=pltpu.CompilerParams(
            dimension_semantics=("arbitrary",), vmem_limit_bytes=VMEM_LIMIT_BYTES),
        name="ada_mod",
    )(c, w_ada, b_ada.reshape(N_MOD, 1, d), *[j.src for j in cast_jobs])
    return outs[0], tuple(outs[1:])


def _ffn_kernel(*refs, transposes, mod_base, alpha):
    n_cast = len(transposes)
    x_ref, mod_ref, wgu_ref, wd_ref, g_ref, b_ref = refs[:6]
    o_ref = refs[6 + n_cast]
    scratch = refs[7 + 2 * n_cast:]
    _run_cast_jobs(refs[6:6 + n_cast], refs[7 + n_cast:7 + 2 * n_cast], transposes)

    d_ff = wd_ref.shape[0]
    tf = FFN_CHUNK
    n_sub = len(scratch) // 2
    batch = pl.ds(pl.program_id(0), 1)
    sh = mod_ref[mod_base, batch, :]
    sc = mod_ref[mod_base + 1, batch, :]
    gate = mod_ref[mod_base + 2, batch, :]
    h_ref, act_ref = scratch
    tm = h_ref.shape[0]
    h_ref[...] = (x_ref[0] * (1.0 + sc) + sh).astype(BF16)

    def gate_up(rows, c):
        lo = c * tf
        g = jnp.dot(h_ref[rows, :], wgu_ref[:, lo:lo + tf], preferred_element_type=F32)
        u = jnp.dot(h_ref[rows, :], wgu_ref[:, d_ff + lo:d_ff + lo + tf],
                    preferred_element_type=F32)
        act_ref[rows, lo:lo + tf] = (_silu(g) * u).astype(BF16)

    gate_up(slice(0, tm), 0)
    row0 = 0
    for sub in FFN_SUB_ROWS:
        rows = slice(row0, row0 + sub)
        row0 = rows.stop
        for c in range(1, d_ff // tf):
            gate_up(rows, c)
        pieces = LAST_PIECES if rows.stop == tm else 1
        step = sub // pieces
        for p in range(pieces):
            part = slice(rows.start + p * step, rows.start + (p + 1) * step)
            down = jnp.dot(act_ref[part, :], wd_ref[...], preferred_element_type=F32)
            y = alpha * x_ref[0, part, :] + (0.5 * (1.0 + gate)) * down
            o_ref[0, part, :] = _layer_norm(y, g_ref[...], b_ref[...])


def _ffn_block(x, mod, w_gate_up, w_down, ln_g, ln_b, *, mod_base, alpha, cast_jobs=()):
    bsz, seq, d = x.shape
    tm = FFN_ROWS
    per_seq = seq // tm
    const2 = lambda b, s: (0, 0)
    cast_in, cast_out, cast_shapes = _cast_plan(
        cast_jobs, bsz * per_seq, lambda b, s: b * per_seq + s)
    outs = pl.pallas_call(
        functools.partial(_ffn_kernel, transposes=tuple(j.transpose for j in cast_jobs),
                          mod_base=mod_base, alpha=alpha),
        grid=(bsz, per_seq),
        in_specs=[
            pl.BlockSpec((1, tm, d), lambda b, s: (b, s, 0)),
            pl.BlockSpec((N_MOD, bsz, d), lambda b, s: (0, 0, 0)),
            pl.BlockSpec((d, 2 * D_FF), const2, pipeline_mode=pl.Buffered(1)),
            pl.BlockSpec((D_FF, d), const2, pipeline_mode=pl.Buffered(1)),
            pl.BlockSpec((1, d), const2),
            pl.BlockSpec((1, d), const2),
        ] + cast_in,
        out_specs=[pl.BlockSpec((1, tm, d), lambda b, s: (b, s, 0))] + cast_out,
        out_shape=[jax.ShapeDtypeStruct((bsz, seq, d), F32)] + cast_shapes,
        scratch_shapes=[pltpu.VMEM((tm, d), BF16), pltpu.VMEM((tm, D_FF), BF16)],
        compiler_params=pltpu.CompilerParams(
            dimension_semantics=("arbitrary", "arbitrary"),
            vmem_limit_bytes=VMEM_LIMIT_BYTES),
        name="ffn_block",
    )(x, mod, w_gate_up, w_down, ln_g.reshape(1, d), ln_b.reshape(1, d),
      *[j.src for j in cast_jobs])
    return outs[0], tuple(outs[1:])


def _mixer_kernel(x_ref, mod_ref, pos_ref, invf_ref, sink_ref, wqkv_ref, wconv_a_ref, wconv_b_ref,
                  cw_ref, wout_ref, g_ref, b_ref, o_ref,
                  k_ref, vt_ref, z_ref, bias_ref, *sub_scratch, alpha):
    ts = x_ref.shape[1]
    n_sub = len(sub_scratch) // MIX_SUB_BUFFERS
    sub = ts // n_sub
    s_idx = pl.program_id(1)
    zpad = SUBLANES
    n_heads_lanes = N_Q_HEADS * WINDOW
    n_slabs = CONV_WIDTH // LANES

    @pl.when(s_idx == 0)
    def _():
        k_ref[0:WINDOW, :] = jnp.zeros((WINDOW, KV_WIDTH), BF16)
        vt_ref[:, 0:WINDOW] = jnp.zeros((KV_WIDTH, WINDOW), BF16)
        z_ref[:, 0:zpad, :] = jnp.zeros((n_slabs, zpad, LANES), F32)

    batch = pl.ds(pl.program_id(0), 1)
    sh = mod_ref[3, batch, :]
    sc = mod_ref[4, batch, :]
    gate = mod_ref[5, batch, :]
    contract_last = (((1,), (1,)), ((), ()))
    contract_first = (((0,), (0,)), ((), ()))
    half = ROT_DIM // 2
    scale = HEAD_DIM ** -0.5 * LOG2_E

    rows2 = 2 * WINDOW
    ki = lax.broadcasted_iota(jnp.int32, (rows2, n_heads_lanes), 0)
    qi = lax.broadcasted_iota(jnp.int32, (rows2, n_heads_lanes), 1) % WINDOW
    bias_ref[...] = jnp.where((ki > qi) & (ki <= qi + WINDOW), 0.0, NEG_BIG)
    no_prev = jnp.where(s_idx == 0, WINDOW, 0)
    first_pen = jnp.where(lax.broadcasted_iota(jnp.int32, (rows2, 1), 0) < no_prev, NEG_BIG, 0.0)
    sink = sink_ref[...] * LOG2_E

    for r in range(n_sub):
        h_ref, pt_ref, qbd_ref, at_ref, pc_ref, cv_ref = sub_scratch[
            r * MIX_SUB_BUFFERS:(r + 1) * MIX_SUB_BUFFERS]
        t0 = r * sub
        x = x_ref[0, t0:t0 + sub, :]
        h_ref[...] = (x * (1.0 + sc) + sh).astype(BF16)
        pt_ref[...] = lax.dot_general(wqkv_ref[...], h_ref[...], contract_last,
                                      preferred_element_type=F32)

        ang = invf_ref[...] * pos_ref[0, :, t0:t0 + sub].astype(F32)
        cos_t = jnp.cos(ang)
        sin_t = jnp.sin(ang)

        def rope(head):
            x1 = head[0:half]
            x2 = head[half:ROT_DIM]
            return jnp.concatenate(
                [x1 * cos_t - x2 * sin_t, x2 * cos_t + x1 * sin_t, head[ROT_DIM:]], axis=0)

        qbd_ref[...] = jnp.zeros(qbd_ref.shape, BF16)
        for hd in range(N_Q_HEADS):
            kv = hd // (N_Q_HEADS // N_KV_HEADS)
            qh = (rope(pt_ref[hd * HEAD_DIM:(hd + 1) * HEAD_DIM, :]) * scale).astype(BF16)
            for n in range(sub // WINDOW):
                qbd_ref[n, kv * HEAD_DIM:(kv + 1) * HEAD_DIM, hd * WINDOW:(hd + 1) * WINDOW] = (
                    qh[:, n * WINDOW:(n + 1) * WINDOW])
        kt = jnp.concatenate(
            [rope(pt_ref[ATTN_WIDTH + kv * HEAD_DIM:ATTN_WIDTH + (kv + 1) * HEAD_DIM, :])
             for kv in range(N_KV_HEADS)], axis=0)
        k_ref[WINDOW + t0:WINDOW + t0 + sub, :] = kt.T.astype(BF16)
        vt_ref[:, WINDOW + t0:WINDOW + t0 + sub] = pt_ref[
            ATTN_WIDTH + KV_WIDTH:ATTN_WIDTH + 2 * KV_WIDTH, :].astype(BF16)

        def conv_proj(c, h_ref=h_ref, pc_ref=pc_ref):
            per_half = wconv_a_ref.shape[1] // MIX_CONV_CHUNK
            w_ref = wconv_a_ref if c < per_half else wconv_b_ref
            src = slice((c % per_half) * MIX_CONV_CHUNK, (c % per_half + 1) * MIX_CONV_CHUNK)
            dst = slice(c * MIX_CONV_CHUNK, (c + 1) * MIX_CONV_CHUNK)
            pc_ref[:, dst] = jnp.dot(h_ref[...], w_ref[:, src], preferred_element_type=F32)

        def scores(n, qbd_ref=qbd_ref, t0=t0):
            r0 = t0 + n * WINDOW
            return jnp.dot(k_ref[r0:r0 + rows2, :], qbd_ref[n], preferred_element_type=F32)

        def attend(n, sco, at_ref=at_ref, t0=t0):
            r0 = t0 + n * WINDOW
            sco = sco + bias_ref[...]
            if r0 == 0:
                sco = sco + first_pen
            mx = jnp.maximum(jnp.max(sco, axis=0, keepdims=True), sink)
            pr = jnp.exp2(sco - mx)
            den = jnp.sum(pr, axis=0, keepdims=True) + jnp.exp2(sink - mx)
            pv = jnp.dot(vt_ref[:, r0:r0 + rows2], pr.astype(BF16), preferred_element_type=F32)
            pv = pv * (1.0 / den)
            for hd in range(N_Q_HEADS):
                kv = hd // (N_Q_HEADS // N_KV_HEADS)
                at_ref[hd * HEAD_DIM:(hd + 1) * HEAD_DIM, n * WINDOW:(n + 1) * WINDOW] = (
                    pv[kv * HEAD_DIM:(kv + 1) * HEAD_DIM,
                       hd * WINDOW:(hd + 1) * WINDOW].astype(BF16))

        n_blocks = sub // WINDOW
        n_chunks = 3 * CONV_WIDTH // MIX_CONV_CHUNK
        pending = scores(0)
        chunk = 0
        for n in range(n_blocks):
            conv_proj(chunk)
            chunk += 1
            upcoming = scores(n + 1) if n + 1 < n_blocks else None
            attend(n, pending)
            pending = upcoming
        while chunk < n_chunks:
            conv_proj(chunk)
            chunk += 1
        pieces = LAST_PIECES if r == n_sub - 1 else 1
        part_rows = sub // pieces

        def attn_proj(p, at_ref=at_ref):
            return lax.dot_general(at_ref[:, p * part_rows:(p + 1) * part_rows],
                                   wout_ref[0:ATTN_WIDTH, :], contract_first,
                                   preferred_element_type=F32)

        first_attn = attn_proj(0)

        z0 = zpad + t0
        for j in range(n_slabs):
            lanes = slice(j * LANES, (j + 1) * LANES)
            u = pc_ref[:, j * LANES:(j + 1) * LANES]
            b_gate = pc_ref[:, CONV_WIDTH + j * LANES:CONV_WIDTH + (j + 1) * LANES]
            c_gate = pc_ref[:, 2 * CONV_WIDTH + j * LANES:2 * CONV_WIDTH + (j + 1) * LANES]
            z_ref[j, z0:z0 + sub, :] = c_gate * u
            y = cw_ref[0:1, lanes] * z_ref[j, z0 - 2:z0 - 2 + sub, :]
            y = y + cw_ref[1:2, lanes] * z_ref[j, z0 - 1:z0 - 1 + sub, :]
            y = y + cw_ref[2:3, lanes] * z_ref[j, z0:z0 + sub, :]
            cv_ref[:, lanes] = (b_gate * y).astype(BF16)

        for p in range(pieces):
            part = slice(p * part_rows, (p + 1) * part_rows)
            mix = (first_attn if p == 0 else attn_proj(p)) + jnp.dot(
                cv_ref[part, :], wout_ref[ATTN_WIDTH:, :], preferred_element_type=F32)
            o_ref[0, t0 + part.start:t0 + part.stop, :] = _layer_norm(
                alpha * x[part] + (1.0 + gate) * mix, g_ref[...], b_ref[...])

    k_ref[0:WINDOW, :] = k_ref[ts:ts + WINDOW, :]
    vt_ref[:, 0:WINDOW] = vt_ref[:, ts:ts + WINDOW]
    z_ref[:, 0:zpad, :] = z_ref[:, ts:ts + zpad, :]


def _mixer_block(x, mod, positions, w_qkv_t, w_conv_a, w_conv_b, conv_w, sinks, w_out, ln_g, ln_b,
                 *, alpha):
    bsz, seq, d = x.shape
    ts = MIX_ROWS
    sub = MIX_SUB_ROWS
    qkv = ATTN_WIDTH + 2 * KV_WIDTH
    conv_half = 3 * CONV_WIDTH // 2
    inv_freq = jnp.power(jnp.float32(ROPE_THETA),
                         -jnp.arange(0, ROT_DIM, 2, dtype=F32) / ROT_DIM)
    sink_row = jnp.repeat(sinks.astype(F32), WINDOW).reshape(1, N_Q_HEADS * WINDOW)
    const2 = lambda b, s: (0, 0)
    sub_scratch = [
        pltpu.VMEM((sub, d), BF16),
        pltpu.VMEM((qkv, sub), F32),
        pltpu.VMEM((sub // WINDOW, KV_WIDTH, N_Q_HEADS * WINDOW), BF16),
        pltpu.VMEM((ATTN_WIDTH, sub), BF16),
        pltpu.VMEM((sub, 3 * CONV_WIDTH), F32),
        pltpu.VMEM((sub, CONV_WIDTH), BF16),
    ]
    assert len(sub_scratch) == MIX_SUB_BUFFERS
    return pl.pallas_call(
        functools.partial(_mixer_kernel, alpha=alpha),
        grid=(bsz, seq // ts),
        in_specs=[
            pl.BlockSpec((1, ts, d), lambda b, s: (b, s, 0)),
            pl.BlockSpec((N_MOD, bsz, d), lambda b, s: (0, 0, 0)),
            pl.BlockSpec((1, 1, ts), lambda b, s: (b, 0, s)),
            pl.BlockSpec((ROT_DIM // 2, 1), const2),
            pl.BlockSpec((1, N_Q_HEADS * WINDOW), const2),
            pl.BlockSpec((qkv, d), const2, pipeline_mode=pl.Buffered(1)),
            pl.BlockSpec((d, conv_half), const2, pipeline_mode=pl.Buffered(1)),
            pl.BlockSpec((d, conv_half), const2, pipeline_mode=pl.Buffered(1)),
            pl.BlockSpec((CONV_W, CONV_WIDTH), const2),
            pl.BlockSpec((d, d), const2, pipeline_mode=pl.Buffered(1)),
            pl.BlockSpec((1, d), const2),
            pl.BlockSpec((1, d), const2),
        ],
        out_specs=pl.BlockSpec((1, ts, d), lambda b, s: (b, s, 0)),
        out_shape=jax.ShapeDtypeStruct((bsz, seq, d), F32),
        scratch_shapes=[
            pltpu.VMEM((ts + WINDOW, KV_WIDTH), BF16),
            pltpu.VMEM((KV_WIDTH, ts + WINDOW), BF16),
            pltpu.VMEM((CONV_WIDTH // LANES, ts + SUBLANES, LANES), F32),
            pltpu.VMEM((2 * WINDOW, N_Q_HEADS * WINDOW), F32),
        ] + sub_scratch * (ts // sub),
        compiler_params=pltpu.CompilerParams(
            dimension_semantics=("arbitrary", "arbitrary"),
            vmem_limit_bytes=VMEM_LIMIT_BYTES),
        name="mixer_block",
    )(x, mod, positions.reshape(bsz, 1, seq), inv_freq.reshape(ROT_DIM // 2, 1), sink_row,
      w_qkv_t, w_conv_a, w_conv_b, conv_w, w_out, ln_g.reshape(1, d), ln_b.reshape(1, d))


def kernel(x, c, positions, w_ada, b_ada, ffn1_w_gate_up, ffn1_w_down, ln1_g, ln1_b, w_in, conv_w, attn_sinks, w_out, ln2_g, ln2_b, ffn2_w_gate_up, ffn2_w_down, ln3_g, ln3_b):
    depth = w_ada.shape[0]
    alpha = (2.0 * depth) ** 0.25
    qkv = ATTN_WIDTH + 2 * KV_WIDTH
    ffn1_w = None
    for l in range(depth):
        first = () if ffn1_w else (_whole(ffn1_w_gate_up[l]), _whole(ffn1_w_down[l]))
        mod, cast = _ada_mod(c, w_ada[l], b_ada[l], cast_jobs=first)
        ffn1_w = ffn1_w or cast
        x, (w_qkv_t, w_conv_a, w_conv_b, w_out_b, ffn2_wgu, ffn2_wd) = _ffn_block(
            x, mod, *ffn1_w, ln1_g[l], ln1_b[l], mod_base=0, alpha=alpha,
            cast_jobs=(_CastJob(w_in[l], 0, qkv, transpose=True),
                       _CastJob(w_in[l], qkv, qkv), _CastJob(w_in[l], 2 * qkv, qkv),
                       _whole(w_out[l]), _whole(ffn2_w_gate_up[l]), _whole(ffn2_w_down[l])))
        x = _mixer_block(x, mod, positions, w_qkv_t, w_conv_a, w_conv_b, conv_w[l], attn_sinks[l],
                         w_out_b, ln2_g[l], ln2_b[l], alpha=alpha)
        next_ffn1 = ((_whole(ffn1_w_gate_up[l + 1]), _whole(ffn1_w_down[l + 1]))
                     if l + 1 < depth else ())
        x, ffn1_w = _ffn_block(x, mod, ffn2_wgu, ffn2_wd, ln3_g[l], ln3_b[l],
                               mod_base=6, alpha=alpha, cast_jobs=next_ffn1)
    return x
```

```python
import functools
from typing import NamedTuple

import jax
import jax.numpy as jnp
from jax import lax
from jax.experimental import pallas as pl
from jax.experimental.pallas import tpu as pltpu

D_MODEL = 1024
HEAD_DIM = 64
ATTN_WIDTH = 512
CONV_WIDTH = 512
N_Q_HEADS = 8
N_KV_HEADS = 2
KV_WIDTH = N_KV_HEADS * HEAD_DIM
WINDOW = 128
ROT_DIM = 16
ROPE_THETA = 500000.0
CONV_W = 3
D_FF = 2816
N_MOD = 9
LN_EPS = 1e-5
IN_WIDTH = ATTN_WIDTH + 2 * KV_WIDTH + 3 * CONV_WIDTH

LANES = 128
SUBLANES = 8
BF16_SUBLANES = 16
VMEM_LIMIT_BYTES = 56 * 1024 * 1024

FFN_ROWS = 1024
FFN_SUB_ROWS = (512, 512)
FFN_CHUNK = 256
LAST_PIECES = 2
MIX_ROWS = 1024
MIX_SUB_ROWS = 512
MIX_SUB_BUFFERS = 4
MIX_CONV_CHUNK = 256
NEG_BIG = -1e30
LOG2_E = 1.4426950408889634

F32 = jnp.float32
BF16 = jnp.bfloat16


def _layer_norm(y, g, b):
    mu = jnp.mean(y, axis=-1, keepdims=True)
    d = y - mu
    var = jnp.mean(d * d, axis=-1, keepdims=True)
    return d * lax.rsqrt(var + LN_EPS) * g + b


def _silu(x):
    return x * jax.nn.sigmoid(x)


class _CastJob(NamedTuple):
    src: jax.Array
    col0: int
    cols: int
    transpose: bool = False


def _whole(w):
    return _CastJob(w, 0, w.shape[1])


def _cast_plan(jobs, n_steps, flat_step):
    in_specs, out_specs, out_shapes = [], [], []
    for job in jobs:
        rows = job.src.shape[0]
        align = LANES if job.transpose else BF16_SUBLANES
        chunk = align
        while rows % chunk or rows // chunk > n_steps:
            chunk += align
        last = rows // chunk - 1
        col_block = job.col0 // job.cols
        assert col_block * job.cols == job.col0

        def chunk_of(*g, last=last):
            return jnp.minimum(flat_step(*g), last)

        in_specs.append(pl.BlockSpec(
            (chunk, job.cols), lambda *g, f=chunk_of, cb=col_block: (f(*g), cb)))
        if job.transpose:
            out_specs.append(pl.BlockSpec((job.cols, chunk), lambda *g, f=chunk_of: (0, f(*g))))
            out_shapes.append(jax.ShapeDtypeStruct((job.cols, rows), BF16))
        else:
            out_specs.append(pl.BlockSpec((chunk, job.cols), lambda *g, f=chunk_of: (f(*g), 0)))
            out_shapes.append(jax.ShapeDtypeStruct((rows, job.cols), BF16))
    return in_specs, out_specs, out_shapes


def _run_cast_jobs(src_refs, dst_refs, transposes):
    for src_ref, dst_ref, transpose in zip(src_refs, dst_refs, transposes):
        w = src_ref[...]
        dst_ref[...] = (w.T if transpose else w).astype(BF16)


def _ada_kernel(*refs, transposes):
    n_cast = len(transposes)
    c_ref, w_ref, b_ref = refs[:3]
    o_ref = refs[3 + n_cast]
    _run_cast_jobs(refs[3:3 + n_cast], refs[4 + n_cast:], transposes)
    cond = _silu(c_ref[...]).astype(BF16)
    o_ref[0] = jnp.dot(cond, w_ref[...].astype(BF16), preferred_element_type=F32) + b_ref[0]


def _ada_mod(c, w_ada, b_ada, cast_jobs=()):
    bsz, d = c.shape
    cast_in, cast_out, cast_shapes = _cast_plan(cast_jobs, N_MOD, lambda j: j)
    outs = pl.pallas_call(
        functools.partial(_ada_kernel, transposes=tuple(j.transpose for j in cast_jobs)),
        grid=(N_MOD,),
        in_specs=[
            pl.BlockSpec((bsz, d), lambda j: (0, 0)),
            pl.BlockSpec((d, d), lambda j: (0, j)),
            pl.BlockSpec((1, 1, d), lambda j: (j, 0, 0)),
        ] + cast_in,
        out_specs=[pl.BlockSpec((1, bsz, d), lambda j: (j, 0, 0))] + cast_out,
        out_shape=[jax.ShapeDtypeStruct((N_MOD, bsz, d), F32)] + cast_shapes,
        compiler_params=pltpu.CompilerParams(
            dimension_semantics=("arbitrary",), vmem_limit_bytes=VMEM_LIMIT_BYTES),
        name="ada_mod",
    )(c, w_ada, b_ada.reshape(N_MOD, 1, d), *[j.src for j in cast_jobs])
    return outs[0], tuple(outs[1:])


def _ffn_kernel(*refs, transposes, mod_base, alpha):
    n_cast = len(transposes)
    x_ref, mod_ref, wgu_ref, wd_ref, g_ref, b_ref = refs[:6]
    o_ref = refs[6 + n_cast]
    scratch = refs[7 + 2 * n_cast:]
    _run_cast_jobs(refs[6:6 + n_cast], refs[7 + n_cast:7 + 2 * n_cast], transposes)

    d_ff = wd_ref.shape[0]
    tf = FFN_CHUNK
    n_sub = len(scratch) // 2
    batch = pl.ds(pl.program_id(0), 1)
    sh = mod_ref[mod_base, batch, :]
    sc = mod_ref[mod_base + 1, batch, :]
    gate = mod_ref[mod_base + 2, batch, :]
    h_ref, act_ref = scratch
    tm = h_ref.shape[0]
    h_ref[...] = (x_ref[0] * (1.0 + sc) + sh).astype(BF16)

    def gate_up(rows, c):
        lo = c * tf
        g = jnp.dot(h_ref[rows, :], wgu_ref[:, lo:lo + tf], preferred_element_type=F32)
        u = jnp.dot(h_ref[rows, :], wgu_ref[:, d_ff + lo:d_ff + lo + tf],
                    preferred_element_type=F32)
        act_ref[rows, lo:lo + tf] = (_silu(g) * u).astype(BF16)

    gate_up(slice(0, tm), 0)
    row0 = 0
    for sub in FFN_SUB_ROWS:
        rows = slice(row0, row0 + sub)
        row0 = rows.stop
        for c in range(1, d_ff // tf):
            gate_up(rows, c)
        pieces = LAST_PIECES if rows.stop == tm else 1
        step = sub // pieces
        for p in range(pieces):
            part = slice(rows.start + p * step, rows.start + (p + 1) * step)
            down = jnp.dot(act_ref[part, :], wd_ref[...], preferred_element_type=F32)
            y = alpha * x_ref[0, part, :] + (0.5 * (1.0 + gate)) * down
            o_ref[0, part, :] = _layer_norm(y, g_ref[...], b_ref[...])


def _ffn_block(x, mod, w_gate_up, w_down, ln_g, ln_b, *, mod_base, alpha, cast_jobs=()):
    bsz, seq, d = x.shape
    tm = FFN_ROWS
    per_seq = seq // tm
    const2 = lambda b, s: (0, 0)
    cast_in, cast_out, cast_shapes = _cast_plan(
        cast_jobs, bsz * per_seq, lambda b, s: b * per_seq + s)
    outs = pl.pallas_call(
        functools.partial(_ffn_kernel, transposes=tuple(j.transpose for j in cast_jobs),
                          mod_base=mod_base, alpha=alpha),
        grid=(bsz, per_seq),
        in_specs=[
            pl.BlockSpec((1, tm, d), lambda b, s: (b, s, 0)),
            pl.BlockSpec((N_MOD, bsz, d), lambda b, s: (0, 0, 0)),
            pl.BlockSpec((d, 2 * D_FF), const2, pipeline_mode=pl.Buffered(1)),
            pl.BlockSpec((D_FF, d), const2, pipeline_mode=pl.Buffered(1)),
            pl.BlockSpec((1, d), const2),
            pl.BlockSpec((1, d), const2),
        ] + cast_in,
        out_specs=[pl.BlockSpec((1, tm, d), lambda b, s: (b, s, 0))] + cast_out,
        out_shape=[jax.ShapeDtypeStruct((bsz, seq, d), F32)] + cast_shapes,
        scratch_shapes=[pltpu.VMEM((tm, d), BF16), pltpu.VMEM((tm, D_FF), BF16)],
        compiler_params=pltpu.CompilerParams(
            dimension_semantics=("arbitrary", "arbitrary"),
            vmem_limit_bytes=VMEM_LIMIT_BYTES),
        name="ffn_block",
    )(x, mod, w_gate_up, w_down, ln_g.reshape(1, d), ln_b.reshape(1, d),
      *[j.src for j in cast_jobs])
    return outs[0], tuple(outs[1:])


def _mixer_kernel(x_ref, mod_ref, pos_ref, invf_ref, sink_ref, wqkv_ref, wconv_a_ref, wconv_b_ref,
                  cw_ref, wout_ref, g_ref, b_ref, o_ref,
                  k_ref, vt_ref, z_ref, bias_ref, hall_ref, ptall_ref, *sub_scratch, alpha):
    ts = x_ref.shape[1]
    n_sub = len(sub_scratch) // MIX_SUB_BUFFERS
    sub = ts // n_sub
    s_idx = pl.program_id(1)
    zpad = SUBLANES
    n_heads_lanes = N_Q_HEADS * WINDOW
    n_slabs = CONV_WIDTH // LANES

    @pl.when(s_idx == 0)
    def _():
        k_ref[0:WINDOW, :] = jnp.zeros((WINDOW, KV_WIDTH), BF16)
        vt_ref[:, 0:WINDOW] = jnp.zeros((KV_WIDTH, WINDOW), BF16)
        z_ref[:, 0:zpad, :] = jnp.zeros((n_slabs, zpad, LANES), F32)

    batch = pl.ds(pl.program_id(0), 1)
    sh = mod_ref[3, batch, :]
    sc = mod_ref[4, batch, :]
    gate = mod_ref[5, batch, :]
    contract_last = (((1,), (1,)), ((), ()))
    contract_first = (((0,), (0,)), ((), ()))
    half = ROT_DIM // 2
    scale = HEAD_DIM ** -0.5 * LOG2_E

    rows2 = 2 * WINDOW
    ki = lax.broadcasted_iota(jnp.int32, (rows2, n_heads_lanes), 0)
    qi = lax.broadcasted_iota(jnp.int32, (rows2, n_heads_lanes), 1) % WINDOW
    bias_ref[...] = jnp.where((ki > qi) & (ki <= qi + WINDOW), 0.0, NEG_BIG)
    no_prev = jnp.where(s_idx == 0, WINDOW, 0)
    first_pen = jnp.where(lax.broadcasted_iota(jnp.int32, (rows2, 1), 0) < no_prev, NEG_BIG, 0.0)
    sink = sink_ref[...] * LOG2_E

    hall_ref[...] = (x_ref[0] * (1.0 + sc) + sh).astype(BF16)
    ptall_ref[...] = lax.dot_general(wqkv_ref[...], hall_ref[...], contract_last,
                                     preferred_element_type=F32)

    for r in range(n_sub):
        qbd_ref, at_ref, pc_ref, cv_ref = sub_scratch[
            r * MIX_SUB_BUFFERS:(r + 1) * MIX_SUB_BUFFERS]
        t0 = r * sub
        x = x_ref[0, t0:t0 + sub, :]
        h_ref = hall_ref.at[t0:t0 + sub, :]
        pt_ref = ptall_ref.at[:, t0:t0 + sub]

        ang = invf_ref[...] * pos_ref[0, :, t0:t0 + sub].astype(F32)
        cos_t = jnp.cos(ang)
        sin_t = jnp.sin(ang)

        def rope(head):
            x1 = head[0:half]
            x2 = head[half:ROT_DIM]
            return jnp.concatenate(
                [x1 * cos_t - x2 * sin_t, x2 * cos_t + x1 * sin_t, head[ROT_DIM:]], axis=0)

        qbd_ref[...] = jnp.zeros(qbd_ref.shape, BF16)
        for hd in range(N_Q_HEADS):
            kv = hd // (N_Q_HEADS // N_KV_HEADS)
            qh = (rope(pt_ref[hd * HEAD_DIM:(hd + 1) * HEAD_DIM, :]) * scale).astype(BF16)
            for n in range(sub // WINDOW):
                qbd_ref[n, kv * HEAD_DIM:(kv + 1) * HEAD_DIM, hd * WINDOW:(hd + 1) * WINDOW] = (
                    qh[:, n * WINDOW:(n + 1) * WINDOW])
        kt = jnp.concatenate(
            [rope(pt_ref[ATTN_WIDTH + kv * HEAD_DIM:ATTN_WIDTH + (kv + 1) * HEAD_DIM, :])
             for kv in range(N_KV_HEADS)], axis=0)
        k_ref[WINDOW + t0:WINDOW + t0 + sub, :] = kt.T.astype(BF16)
        vt_ref[:, WINDOW + t0:WINDOW + t0 + sub] = pt_ref[
            ATTN_WIDTH + KV_WIDTH:ATTN_WIDTH + 2 * KV_WIDTH, :].astype(BF16)

        def conv_proj(c, h_ref=h_ref, pc_ref=pc_ref):
            per_half = wconv_a_ref.shape[1] // MIX_CONV_CHUNK
            w_ref = wconv_a_ref if c < per_half else wconv_b_ref
            src = slice((c % per_half) * MIX_CONV_CHUNK, (c % per_half + 1) * MIX_CONV_CHUNK)
            dst = slice(c * MIX_CONV_CHUNK, (c + 1) * MIX_CONV_CHUNK)
            pc_ref[:, dst] = jnp.dot(h_ref[...], w_ref[:, src], preferred_element_type=F32)

        def scores(n, qbd_ref=qbd_ref, t0=t0):
            r0 = t0 + n * WINDOW
            return jnp.dot(k_ref[r0:r0 + rows2, :], qbd_ref[n], preferred_element_type=F32)

        def attend(n, sco, at_ref=at_ref, t0=t0):
            r0 = t0 + n * WINDOW
            sco = sco + bias_ref[...]
            if r0 == 0:
                sco = sco + first_pen
            mx = jnp.maximum(jnp.max(sco, axis=0, keepdims=True), sink)
            pr = jnp.exp2(sco - mx)
            den = jnp.sum(pr, axis=0, keepdims=True) + jnp.exp2(sink - mx)
            pv = jnp.dot(vt_ref[:, r0:r0 + rows2], pr.astype(BF16), preferred_element_type=F32)
            pv = pv * (1.0 / den)
            for hd in range(N_Q_HEADS):
                kv = hd // (N_Q_HEADS // N_KV_HEADS)
                at_ref[hd * HEAD_DIM:(hd + 1) * HEAD_DIM, n * WINDOW:(n + 1) * WINDOW] = (
                    pv[kv * HEAD_DIM:(kv + 1) * HEAD_DIM,
                       hd * WINDOW:(hd + 1) * WINDOW].astype(BF16))

        n_blocks = sub // WINDOW
        n_chunks = 3 * CONV_WIDTH // MIX_CONV_CHUNK
        chunk = 0
        for n in range(n_blocks):
            pending = scores(n)
            conv_proj(chunk)
            chunk += 1
            attend(n, pending)
        while chunk < n_chunks:
            conv_proj(chunk)
            chunk += 1
        pieces = LAST_PIECES if r == n_sub - 1 else 1
        part_rows = sub // pieces

        def attn_proj(p, at_ref=at_ref):
            return lax.dot_general(at_ref[:, p * part_rows:(p + 1) * part_rows],
                                   wout_ref[0:ATTN_WIDTH, :], contract_first,
                                   preferred_element_type=F32)

        first_attn = attn_proj(0)

        z0 = zpad + t0
        for j in range(n_slabs):
            lanes = slice(j * LANES, (j + 1) * LANES)
            u = pc_ref[:, j * LANES:(j + 1) * LANES]
            b_gate = pc_ref[:, CONV_WIDTH + j * LANES:CONV_WIDTH + (j + 1) * LANES]
            c_gate = pc_ref[:, 2 * CONV_WIDTH + j * LANES:2 * CONV_WIDTH + (j + 1) * LANES]
            z_ref[j, z0:z0 + sub, :] = c_gate * u
            y = cw_ref[0:1, lanes] * z_ref[j, z0 - 2:z0 - 2 + sub, :]
            y = y + cw_ref[1:2, lanes] * z_ref[j, z0 - 1:z0 - 1 + sub, :]
            y = y + cw_ref[2:3, lanes] * z_ref[j, z0:z0 + sub, :]
            cv_ref[:, lanes] = (b_gate * y).astype(BF16)

        for p in range(pieces):
            part = slice(p * part_rows, (p + 1) * part_rows)
            mix = (first_attn if p == 0 else attn_proj(p)) + jnp.dot(
                cv_ref[part, :], wout_ref[ATTN_WIDTH:, :], preferred_element_type=F32)
            o_ref[0, t0 + part.start:t0 + part.stop, :] = _layer_norm(
                alpha * x[part] + (1.0 + gate) * mix, g_ref[...], b_ref[...])

    k_ref[0:WINDOW, :] = k_ref[ts:ts + WINDOW, :]
    vt_ref[:, 0:WINDOW] = vt_ref[:, ts:ts + WINDOW]
    z_ref[:, 0:zpad, :] = z_ref[:, ts:ts + zpad, :]


def _mixer_block(x, mod, positions, w_qkv_t, w_conv_a, w_conv_b, conv_w, sinks, w_out, ln_g, ln_b,
                 *, alpha):
    bsz, seq, d = x.shape
    ts = MIX_ROWS
    sub = MIX_SUB_ROWS
    qkv = ATTN_WIDTH + 2 * KV_WIDTH
    conv_half = 3 * CONV_WIDTH // 2
    inv_freq = jnp.power(jnp.float32(ROPE_THETA),
                         -jnp.arange(0, ROT_DIM, 2, dtype=F32) / ROT_DIM)
    sink_row = jnp.repeat(sinks.astype(F32), WINDOW).reshape(1, N_Q_HEADS * WINDOW)
    const2 = lambda b, s: (0, 0)
    sub_scratch = [
        pltpu.VMEM((sub // WINDOW, KV_WIDTH, N_Q_HEADS * WINDOW), BF16),
        pltpu.VMEM((ATTN_WIDTH, sub), BF16),
        pltpu.VMEM((sub, 3 * CONV_WIDTH), F32),
        pltpu.VMEM((sub, CONV_WIDTH), BF16),
    ]
    assert len(sub_scratch) == MIX_SUB_BUFFERS
    return pl.pallas_call(
        functools.partial(_mixer_kernel, alpha=alpha),
        grid=(bsz, seq // ts),
        in_specs=[
            pl.BlockSpec((1, ts, d), lambda b, s: (b, s, 0)),
            pl.BlockSpec((N_MOD, bsz, d), lambda b, s: (0, 0, 0)),
            pl.BlockSpec((1, 1, ts), lambda b, s: (b, 0, s)),
            pl.BlockSpec((ROT_DIM // 2, 1), const2),
            pl.BlockSpec((1, N_Q_HEADS * WINDOW), const2),
            pl.BlockSpec((qkv, d), const2, pipeline_mode=pl.Buffered(1)),
            pl.BlockSpec((d, conv_half), const2, pipeline_mode=pl.Buffered(1)),
            pl.BlockSpec((d, conv_half), const2, pipeline_mode=pl.Buffered(1)),
            pl.BlockSpec((CONV_W, CONV_WIDTH), const2),
            pl.BlockSpec((d, d), const2, pipeline_mode=pl.Buffered(1)),
            pl.BlockSpec((1, d), const2),
            pl.BlockSpec((1, d), const2),
        ],
        out_specs=pl.BlockSpec((1, ts, d), lambda b, s: (b, s, 0)),
        out_shape=jax.ShapeDtypeStruct((bsz, seq, d), F32),
        scratch_shapes=[
            pltpu.VMEM((ts + WINDOW, KV_WIDTH), BF16),
            pltpu.VMEM((KV_WIDTH, ts + WINDOW), BF16),
            pltpu.VMEM((CONV_WIDTH // LANES, ts + SUBLANES, LANES), F32),
            pltpu.VMEM((2 * WINDOW, N_Q_HEADS * WINDOW), F32),
            pltpu.VMEM((ts, d), BF16),
            pltpu.VMEM((qkv, ts), F32),
        ] + sub_scratch * (ts // sub),
        compiler_params=pltpu.CompilerParams(
            dimension_semantics=("arbitrary", "arbitrary"),
            vmem_limit_bytes=VMEM_LIMIT_BYTES),
        name="mixer_block",
    )(x, mod, positions.reshape(bsz, 1, seq), inv_freq.reshape(ROT_DIM // 2, 1), sink_row,
      w_qkv_t, w_conv_a, w_conv_b, conv_w, w_out, ln_g.reshape(1, d), ln_b.reshape(1, d))


def kernel(x, c, positions, w_ada, b_ada, ffn1_w_gate_up, ffn1_w_down, ln1_g, ln1_b, w_in, conv_w, attn_sinks, w_out, ln2_g, ln2_b, ffn2_w_gate_up, ffn2_w_down, ln3_g, ln3_b):
    depth = w_ada.shape[0]
    alpha = (2.0 * depth) ** 0.25
    qkv = ATTN_WIDTH + 2 * KV_WIDTH
    ffn1_w = None
    for l in range(depth):
        first = () if ffn1_w else (_whole(ffn1_w_gate_up[l]), _whole(ffn1_w_down[l]))
        mod, cast = _ada_mod(c, w_ada[l], b_ada[l], cast_jobs=first)
        ffn1_w = ffn1_w or cast
        x, (w_qkv_t, w_conv_a, w_conv_b, w_out_b, ffn2_wgu, ffn2_wd) = _ffn_block(
            x, mod, *ffn1_w, ln1_g[l], ln1_b[l], mod_base=0, alpha=alpha,
            cast_jobs=(_CastJob(w_in[l], 0, qkv, transpose=True),
                       _CastJob(w_in[l], qkv, qkv), _CastJob(w_in[l], 2 * qkv, qkv),
                       _whole(w_out[l]), _whole(ffn2_w_gate_up[l]), _whole(ffn2_w_down[l])))
        x = _mixer_block(x, mod, positions, w_qkv_t, w_conv_a, w_conv_b, conv_w[l], attn_sinks[l],
                         w_out_b, ln2_g[l], ln2_b[l], alpha=alpha)
        next_ffn1 = ((_whole(ffn1_w_gate_up[l + 1]), _whole(ffn1_w_down[l + 1]))
                     if l + 1 < depth else ())
        x, ffn1_w = _ffn_block(x, mod, ffn2_wgu, ffn2_wd, ln3_g[l], ln3_b[l],
                               mod_base=6, alpha=alpha, cast_jobs=next_ffn1)
    return x
```

```python
import functools
from typing import NamedTuple

import jax
import jax.numpy as jnp
from jax import lax
from jax.experimental import pallas as pl
from jax.experimental.pallas import tpu as pltpu

D_MODEL = 1024
HEAD_DIM = 64
ATTN_WIDTH = 512
CONV_WIDTH = 512
N_Q_HEADS = 8
N_KV_HEADS = 2
KV_WIDTH = N_KV_HEADS * HEAD_DIM
WINDOW = 128
ROT_DIM = 16
ROPE_THETA = 500000.0
CONV_W = 3
D_FF = 2816
N_MOD = 9
LN_EPS = 1e-5
IN_WIDTH = ATTN_WIDTH + 2 * KV_WIDTH + 3 * CONV_WIDTH

LANES = 128
SUBLANES = 8
BF16_SUBLANES = 16
VMEM_LIMIT_BYTES = 56 * 1024 * 1024

FFN_ROWS = 1024
FFN_SUB_ROWS = (512, 512)
FFN_CHUNK = 256
LAST_PIECES = 2
DOWN_PIECES = 2
MIX_ROWS = 1024
MIX_SUB_ROWS = 512
MIX_SUB_BUFFERS = 4
MIX_CONV_CHUNK = 256
NEG_BIG = -1e30
LOG2_E = 1.4426950408889634

F32 = jnp.float32
BF16 = jnp.bfloat16


def _layer_norm(y, g, b):
    mu = jnp.mean(y, axis=-1, keepdims=True)
    d = y - mu
    var = jnp.mean(d * d, axis=-1, keepdims=True)
    return d * lax.rsqrt(var + LN_EPS) * g + b


def _silu(x):
    return x * jax.nn.sigmoid(x)


class _CastJob(NamedTuple):
    src: jax.Array
    col0: int
    cols: int
    transpose: bool = False


def _whole(w):
    return _CastJob(w, 0, w.shape[1])


def _cast_plan(jobs, n_steps, flat_step):
    in_specs, out_specs, out_shapes = [], [], []
    for job in jobs:
        rows = job.src.shape[0]
        align = LANES if job.transpose else BF16_SUBLANES
        chunk = align
        while rows % chunk or rows // chunk > n_steps:
            chunk += align
        last = rows // chunk - 1
        col_block = job.col0 // job.cols
        assert col_block * job.cols == job.col0

        def chunk_of(*g, last=last):
            return jnp.minimum(flat_step(*g), last)

        in_specs.append(pl.BlockSpec(
            (chunk, job.cols), lambda *g, f=chunk_of, cb=col_block: (f(*g), cb)))
        if job.transpose:
            out_specs.append(pl.BlockSpec((job.cols, chunk), lambda *g, f=chunk_of: (0, f(*g))))
            out_shapes.append(jax.ShapeDtypeStruct((job.cols, rows), BF16))
        else:
            out_specs.append(pl.BlockSpec((chunk, job.cols), lambda *g, f=chunk_of: (f(*g), 0)))
            out_shapes.append(jax.ShapeDtypeStruct((rows, job.cols), BF16))
    return in_specs, out_specs, out_shapes


def _run_cast_jobs(src_refs, dst_refs, transposes):
    for src_ref, dst_ref, transpose in zip(src_refs, dst_refs, transposes):
        w = src_ref[...]
        dst_ref[...] = (w.T if transpose else w).astype(BF16)


def _ada_kernel(*refs, transposes):
    n_cast = len(transposes)
    c_ref, w_ref, b_ref = refs[:3]
    o_ref = refs[3 + n_cast]
    _run_cast_jobs(refs[3:3 + n_cast], refs[4 + n_cast:], transposes)
    cond = _silu(c_ref[...]).astype(BF16)
    o_ref[0] = jnp.dot(cond, w_ref[...].astype(BF16), preferred_element_type=F32) + b_ref[0]


def _ada_mod(c, w_ada, b_ada, cast_jobs=()):
    bsz, d = c.shape
    cast_in, cast_out, cast_shapes = _cast_plan(cast_jobs, N_MOD, lambda j: j)
    outs = pl.pallas_call(
        functools.partial(_ada_kernel, transposes=tuple(j.transpose for j in cast_jobs)),
        grid=(N_MOD,),
        in_specs=[
            pl.BlockSpec((bsz, d), lambda j: (0, 0)),
            pl.BlockSpec((d, d), lambda j: (0, j)),
            pl.BlockSpec((1, 1, d), lambda j: (j, 0, 0)),
        ] + cast_in,
        out_specs=[pl.BlockSpec((1, bsz, d), lambda j: (j, 0, 0))] + cast_out,
        out_shape=[jax.ShapeDtypeStruct((N_MOD, bsz, d), F32)] + cast_shapes,
        compiler_params=pltpu.CompilerParams(
            dimension_semantics=("arbitrary",), vmem_limit_bytes=VMEM_LIMIT_BYTES),
        name="ada_mod",
    )(c, w_ada, b_ada.reshape(N_MOD, 1, d), *[j.src for j in cast_jobs])
    return outs[0], tuple(outs[1:])


def _ffn_kernel(*refs, transposes, mod_base, alpha):
    n_cast = len(transposes)
    x_ref, mod_ref, wgu_ref, wd_ref, g_ref, b_ref = refs[:6]
    o_ref = refs[6 + n_cast]
    scratch = refs[7 + 2 * n_cast:]
    _run_cast_jobs(refs[6:6 + n_cast], refs[7 + n_cast:7 + 2 * n_cast], transposes)

    d_ff = wd_ref.shape[0]
    tf = FFN_CHUNK
    n_sub = len(scratch) // 2
    batch = pl.ds(pl.program_id(0), 1)
    sh = mod_ref[mod_base, batch, :]
    sc = mod_ref[mod_base + 1, batch, :]
    gate = mod_ref[mod_base + 2, batch, :]
    h_ref, act_ref = scratch
    tm = h_ref.shape[0]
    h_ref[...] = (x_ref[0] * (1.0 + sc) + sh).astype(BF16)

    def gate_up(rows, c):
        lo = c * tf
        g = jnp.dot(h_ref[rows, :], wgu_ref[:, lo:lo + tf], preferred_element_type=F32)
        u = jnp.dot(h_ref[rows, :], wgu_ref[:, d_ff + lo:d_ff + lo + tf],
                    preferred_element_type=F32)
        act_ref[rows, lo:lo + tf] = (_silu(g) * u).astype(BF16)

    gate_up(slice(0, tm), 0)
    row0 = 0
    for sub in FFN_SUB_ROWS:
        rows = slice(row0, row0 + sub)
        row0 = rows.stop
        for c in range(1, d_ff // tf):
            gate_up(rows, c)
        pieces = DOWN_PIECES
        step = sub // pieces
        for p in range(pieces):
            part = slice(rows.start + p * step, rows.start + (p + 1) * step)
            down = jnp.dot(act_ref[part, :], wd_ref[...], preferred_element_type=F32)
            y = alpha * x_ref[0, part, :] + (0.5 * (1.0 + gate)) * down
            o_ref[0, part, :] = _layer_norm(y, g_ref[...], b_ref[...])


def _ffn_block(x, mod, w_gate_up, w_down, ln_g, ln_b, *, mod_base, alpha, cast_jobs=()):
    bsz, seq, d = x.shape
    tm = FFN_ROWS
    per_seq = seq // tm
    const2 = lambda b, s: (0, 0)
    cast_in, cast_out, cast_shapes = _cast_plan(
        cast_jobs, bsz * per_seq, lambda b, s: b * per_seq + s)
    outs = pl.pallas_call(
        functools.partial(_ffn_kernel, transposes=tuple(j.transpose for j in cast_jobs),
                          mod_base=mod_base, alpha=alpha),
        grid=(bsz, per_seq),
        in_specs=[
            pl.BlockSpec((1, tm, d), lambda b, s: (b, s, 0)),
            pl.BlockSpec((N_MOD, bsz, d), lambda b, s: (0, 0, 0)),
            pl.BlockSpec((d, 2 * D_FF), const2, pipeline_mode=pl.Buffered(1)),
            pl.BlockSpec((D_FF, d), const2, pipeline_mode=pl.Buffered(1)),
            pl.BlockSpec((1, d), const2),
            pl.BlockSpec((1, d), const2),
        ] + cast_in,
        out_specs=[pl.BlockSpec((1, tm, d), lambda b, s: (b, s, 0))] + cast_out,
        out_shape=[jax.ShapeDtypeStruct((bsz, seq, d), F32)] + cast_shapes,
        scratch_shapes=[pltpu.VMEM((tm, d), BF16), pltpu.VMEM((tm, D_FF), BF16)],
        compiler_params=pltpu.CompilerParams(
            dimension_semantics=("arbitrary", "arbitrary"),
            vmem_limit_bytes=VMEM_LIMIT_BYTES),
        name="ffn_block",
    )(x, mod, w_gate_up, w_down, ln_g.reshape(1, d), ln_b.reshape(1, d),
      *[j.src for j in cast_jobs])
    return outs[0], tuple(outs[1:])


def _mixer_kernel(x_ref, mod_ref, pos_ref, invf_ref, sink_ref, wqkv_ref, wconv_a_ref, wconv_b_ref,
                  cw_ref, wout_ref, g_ref, b_ref, bias_ref, o_ref,
                  k_ref, vt_ref, z_ref, hall_ref, ptall_ref, *sub_scratch, alpha):
    ts = x_ref.shape[1]
    n_sub = len(sub_scratch) // MIX_SUB_BUFFERS
    sub = ts // n_sub
    s_idx = pl.program_id(1)
    zpad = SUBLANES
    n_slabs = CONV_WIDTH // LANES

    @pl.when(s_idx == 0)
    def _():
        k_ref[0:WINDOW, :] = jnp.zeros((WINDOW, KV_WIDTH), BF16)
        vt_ref[:, 0:WINDOW] = jnp.zeros((KV_WIDTH, WINDOW), BF16)
        z_ref[:, 0:zpad, :] = jnp.zeros((n_slabs, zpad, LANES), F32)

    batch = pl.ds(pl.program_id(0), 1)
    sh = mod_ref[3, batch, :]
    sc = mod_ref[4, batch, :]
    gate = mod_ref[5, batch, :]
    contract_last = (((1,), (1,)), ((), ()))
    contract_first = (((0,), (0,)), ((), ()))
    half = ROT_DIM // 2
    scale = HEAD_DIM ** -0.5 * LOG2_E

    rows2 = 2 * WINDOW
    no_prev = jnp.where(s_idx == 0, WINDOW, 0)
    first_pen = jnp.where(lax.broadcasted_iota(jnp.int32, (rows2, 1), 0) < no_prev, NEG_BIG, 0.0)
    sink = sink_ref[...] * LOG2_E

    hall_ref[...] = (x_ref[0] * (1.0 + sc) + sh).astype(BF16)
    ptall_ref[...] = lax.dot_general(wqkv_ref[...], hall_ref[...], contract_last,
                                     preferred_element_type=F32)

    for r in range(n_sub):
        qbd_ref, at_ref, pc_ref, cv_ref = sub_scratch[
            r * MIX_SUB_BUFFERS:(r + 1) * MIX_SUB_BUFFERS]
        t0 = r * sub
        x = x_ref[0, t0:t0 + sub, :]
        h_ref = hall_ref.at[t0:t0 + sub, :]
        pt_ref = ptall_ref.at[:, t0:t0 + sub]

        ang = invf_ref[...] * pos_ref[0, :, t0:t0 + sub].astype(F32)
        cos_t = jnp.cos(ang)
        sin_t = jnp.sin(ang)

        def rope(head):
            x1 = head[0:half]
            x2 = head[half:ROT_DIM]
            return jnp.concatenate(
                [x1 * cos_t - x2 * sin_t, x2 * cos_t + x1 * sin_t, head[ROT_DIM:]], axis=0)

        qbd_ref[...] = jnp.zeros(qbd_ref.shape, BF16)
        for hd in range(N_Q_HEADS):
            kv = hd // (N_Q_HEADS // N_KV_HEADS)
            qh = (rope(pt_ref[hd * HEAD_DIM:(hd + 1) * HEAD_DIM, :]) * scale).astype(BF16)
            for n in range(sub // WINDOW):
                qbd_ref[n, kv * HEAD_DIM:(kv + 1) * HEAD_DIM, hd * WINDOW:(hd + 1) * WINDOW] = (
                    qh[:, n * WINDOW:(n + 1) * WINDOW])
        kt = jnp.concatenate(
            [rope(pt_ref[ATTN_WIDTH + kv * HEAD_DIM:ATTN_WIDTH + (kv + 1) * HEAD_DIM, :])
             for kv in range(N_KV_HEADS)], axis=0)
        k_ref[WINDOW + t0:WINDOW + t0 + sub, :] = kt.T.astype(BF16)
        vt_ref[:, WINDOW + t0:WINDOW + t0 + sub] = pt_ref[
            ATTN_WIDTH + KV_WIDTH:ATTN_WIDTH + 2 * KV_WIDTH, :].astype(BF16)

        def conv_proj(c, h_ref=h_ref, pc_ref=pc_ref):
            per_half = wconv_a_ref.shape[1] // MIX_CONV_CHUNK
            w_ref = wconv_a_ref if c < per_half else wconv_b_ref
            src = slice((c % per_half) * MIX_CONV_CHUNK, (c % per_half + 1) * MIX_CONV_CHUNK)
            dst = slice(c * MIX_CONV_CHUNK, (c + 1) * MIX_CONV_CHUNK)
            pc_ref[:, dst] = jnp.dot(h_ref[...], w_ref[:, src], preferred_element_type=F32)

        def scores(n, qbd_ref=qbd_ref, t0=t0):
            r0 = t0 + n * WINDOW
            return jnp.dot(k_ref[r0:r0 + rows2, :], qbd_ref[n], preferred_element_type=F32)

        def attend(n, sco, at_ref=at_ref, t0=t0):
            r0 = t0 + n * WINDOW
            sco = sco + bias_ref[...]
            if r0 == 0:
                sco = sco + first_pen
            mx = jnp.maximum(jnp.max(sco, axis=0, keepdims=True), sink)
            pr = jnp.exp2(sco - mx)
            den = jnp.sum(pr, axis=0, keepdims=True) + jnp.exp2(sink - mx)
            pv = jnp.dot(vt_ref[:, r0:r0 + rows2], pr.astype(BF16), preferred_element_type=F32)
            pv = pv * (1.0 / den)
            for hd in range(N_Q_HEADS):
                kv = hd // (N_Q_HEADS // N_KV_HEADS)
                at_ref[hd * HEAD_DIM:(hd + 1) * HEAD_DIM, n * WINDOW:(n + 1) * WINDOW] = (
                    pv[kv * HEAD_DIM:(kv + 1) * HEAD_DIM,
                       hd * WINDOW:(hd + 1) * WINDOW].astype(BF16))

        n_blocks = sub // WINDOW
        n_chunks = 3 * CONV_WIDTH // MIX_CONV_CHUNK
        chunk = 0
        for n in range(n_blocks):
            pending = scores(n)
            conv_proj(chunk)
            chunk += 1
            attend(n, pending)
        while chunk < n_chunks:
            conv_proj(chunk)
            chunk += 1
        pieces = LAST_PIECES if r == n_sub - 1 else 1
        part_rows = sub // pieces

        def attn_proj(p, at_ref=at_ref):
            return lax.dot_general(at_ref[:, p * part_rows:(p + 1) * part_rows],
                                   wout_ref[0:ATTN_WIDTH, :], contract_first,
                                   preferred_element_type=F32)

        first_attn = attn_proj(0)

        z0 = zpad + t0
        for j in range(n_slabs):
            lanes = slice(j * LANES, (j + 1) * LANES)
            u = pc_ref[:, j * LANES:(j + 1) * LANES]
            b_gate = pc_ref[:, CONV_WIDTH + j * LANES:CONV_WIDTH + (j + 1) * LANES]
            c_gate = pc_ref[:, 2 * CONV_WIDTH + j * LANES:2 * CONV_WIDTH + (j + 1) * LANES]
            z_ref[j, z0:z0 + sub, :] = c_gate * u
            y = cw_ref[0:1, lanes] * z_ref[j, z0 - 2:z0 - 2 + sub, :]
            y = y + cw_ref[1:2, lanes] * z_ref[j, z0 - 1:z0 - 1 + sub, :]
            y = y + cw_ref[2:3, lanes] * z_ref[j, z0:z0 + sub, :]
            cv_ref[:, lanes] = (b_gate * y).astype(BF16)

        for p in range(pieces):
            part = slice(p * part_rows, (p + 1) * part_rows)
            mix = (first_attn if p == 0 else attn_proj(p)) + jnp.dot(
                cv_ref[part, :], wout_ref[ATTN_WIDTH:, :], preferred_element_type=F32)
            o_ref[0, t0 + part.start:t0 + part.stop, :] = _layer_norm(
                alpha * x[part] + (1.0 + gate) * mix, g_ref[...], b_ref[...])

    k_ref[0:WINDOW, :] = k_ref[ts:ts + WINDOW, :]
    vt_ref[:, 0:WINDOW] = vt_ref[:, ts:ts + WINDOW]
    z_ref[:, 0:zpad, :] = z_ref[:, ts:ts + zpad, :]


def _mixer_block(x, mod, positions, w_qkv_t, w_conv_a, w_conv_b, conv_w, sinks, w_out, ln_g, ln_b,
                 *, alpha):
    bsz, seq, d = x.shape
    ts = MIX_ROWS
    sub = MIX_SUB_ROWS
    qkv = ATTN_WIDTH + 2 * KV_WIDTH
    conv_half = 3 * CONV_WIDTH // 2
    inv_freq = jnp.power(jnp.float32(ROPE_THETA),
                         -jnp.arange(0, ROT_DIM, 2, dtype=F32) / ROT_DIM)
    sink_row = jnp.repeat(sinks.astype(F32), WINDOW).reshape(1, N_Q_HEADS * WINDOW)
    ki = lax.broadcasted_iota(jnp.int32, (2 * WINDOW, N_Q_HEADS * WINDOW), 0)
    qi = lax.broadcasted_iota(jnp.int32, (2 * WINDOW, N_Q_HEADS * WINDOW), 1) % WINDOW
    band_bias = jnp.where((ki > qi) & (ki <= qi + WINDOW), 0.0, NEG_BIG).astype(F32)
    const2 = lambda b, s: (0, 0)
    sub_scratch = [
        pltpu.VMEM((sub // WINDOW, KV_WIDTH, N_Q_HEADS * WINDOW), BF16),
        pltpu.VMEM((ATTN_WIDTH, sub), BF16),
        pltpu.VMEM((sub, 3 * CONV_WIDTH), F32),
        pltpu.VMEM((sub, CONV_WIDTH), BF16),
    ]
    assert len(sub_scratch) == MIX_SUB_BUFFERS
    return pl.pallas_call(
        functools.partial(_mixer_kernel, alpha=alpha),
        grid=(bsz, seq // ts),
        in_specs=[
            pl.BlockSpec((1, ts, d), lambda b, s: (b, s, 0)),
            pl.BlockSpec((N_MOD, bsz, d), lambda b, s: (0, 0, 0)),
            pl.BlockSpec((1, 1, ts), lambda b, s: (b, 0, s)),
            pl.BlockSpec((ROT_DIM // 2, 1), const2),
            pl.BlockSpec((1, N_Q_HEADS * WINDOW), const2),
            pl.BlockSpec((qkv, d), const2, pipeline_mode=pl.Buffered(1)),
            pl.BlockSpec((d, conv_half), const2, pipeline_mode=pl.Buffered(1)),
            pl.BlockSpec((d, conv_half), const2, pipeline_mode=pl.Buffered(1)),
            pl.BlockSpec((CONV_W, CONV_WIDTH), const2),
            pl.BlockSpec((d, d), const2, pipeline_mode=pl.Buffered(1)),
            pl.BlockSpec((1, d), const2),
            pl.BlockSpec((1, d), const2),
            pl.BlockSpec((2 * WINDOW, N_Q_HEADS * WINDOW), const2, pipeline_mode=pl.Buffered(1)),
        ],
        out_specs=pl.BlockSpec((1, ts, d), lambda b, s: (b, s, 0)),
        out_shape=jax.ShapeDtypeStruct((bsz, seq, d), F32),
        scratch_shapes=[
            pltpu.VMEM((ts + WINDOW, KV_WIDTH), BF16),
            pltpu.VMEM((KV_WIDTH, ts + WINDOW), BF16),
            pltpu.VMEM((CONV_WIDTH // LANES, ts + SUBLANES, LANES), F32),
            pltpu.VMEM((ts, d), BF16),
            pltpu.VMEM((qkv, ts), F32),
        ] + sub_scratch * (ts // sub),
        compiler_params=pltpu.CompilerParams(
            dimension_semantics=("arbitrary", "arbitrary"),
            vmem_limit_bytes=VMEM_LIMIT_BYTES),
        name="mixer_block",
    )(x, mod, positions.reshape(bsz, 1, seq), inv_freq.reshape(ROT_DIM // 2, 1), sink_row,
      w_qkv_t, w_conv_a, w_conv_b, conv_w, w_out, ln_g.reshape(1, d), ln_b.reshape(1, d),
      band_bias)


def kernel(x, c, positions, w_ada, b_ada, ffn1_w_gate_up, ffn1_w_down, ln1_g, ln1_b, w_in, conv_w, attn_sinks, w_out, ln2_g, ln2_b, ffn2_w_gate_up, ffn2_w_down, ln3_g, ln3_b):
    depth = w_ada.shape[0]
    alpha = (2.0 * depth) ** 0.25
    qkv = ATTN_WIDTH + 2 * KV_WIDTH
    ffn1_w = None
    for l in range(depth):
        first = () if ffn1_w else (_whole(ffn1_w_gate_up[l]), _whole(ffn1_w_down[l]))
        mod, cast = _ada_mod(c, w_ada[l], b_ada[l], cast_jobs=first)
        ffn1_w = ffn1_w or cast
        x, (w_qkv_t, w_conv_a, w_conv_b, w_out_b, ffn2_wgu, ffn2_wd) = _ffn_block(
            x, mod, *ffn1_w, ln1_g[l], ln1_b[l], mod_base=0, alpha=alpha,
            cast_jobs=(_CastJob(w_in[l], 0, qkv, transpose=True),
                       _CastJob(w_in[l], qkv, qkv), _CastJob(w_in[l], 2 * qkv, qkv),
                       _whole(w_out[l]), _whole(ffn2_w_gate_up[l]), _whole(ffn2_w_down[l])))
        x = _mixer_block(x, mod, positions, w_qkv_t, w_conv_a, w_conv_b, conv_w[l], attn_sinks[l],
                         w_out_b, ln2_g[l], ln2_b[l], alpha=alpha)
        next_ffn1 = ((_whole(ffn1_w_gate_up[l + 1]), _whole(ffn1_w_down[l + 1]))
                     if l + 1 < depth else ())
        x, ffn1_w = _ffn_block(x, mod, ffn2_wgu, ffn2_wd, ln3_g[l], ln3_b[l],
                               mod_base=6, alpha=alpha, cast_jobs=next_ffn1)
    return x
```

```python
import functools
from typing import NamedTuple

import jax
import jax.numpy as jnp
from jax import lax
from jax.experimental import pallas as pl
from jax.experimental.pallas import tpu as pltpu

D_MODEL = 1024
HEAD_DIM = 64
ATTN_WIDTH = 512
CONV_WIDTH = 512
N_Q_HEADS = 8
N_KV_HEADS = 2
KV_WIDTH = N_KV_HEADS * HEAD_DIM
WINDOW = 128
ROT_DIM = 16
ROPE_THETA = 500000.0
CONV_W = 3
D_FF = 2816
N_MOD = 9
LN_EPS = 1e-5
IN_WIDTH = ATTN_WIDTH + 2 * KV_WIDTH + 3 * CONV_WIDTH

LANES = 128
SUBLANES = 8
BF16_SUBLANES = 16
VMEM_LIMIT_BYTES = 56 * 1024 * 1024

FFN_ROWS = 1024
FFN_SUB_ROWS = (512, 512)
FFN_CHUNK = 256
LAST_PIECES = 2
DOWN_PIECES = 2
MIX_ROWS = 1024
MIX_SUB_ROWS = 512
MIX_SUB_BUFFERS = 3
MIX_CONV_CHUNK = 256
NEG_BIG = -1e30
LOG2_E = 1.4426950408889634

F32 = jnp.float32
BF16 = jnp.bfloat16


def _layer_norm(y, g, b):
    mu = jnp.mean(y, axis=-1, keepdims=True)
    d = y - mu
    var = jnp.mean(d * d, axis=-1, keepdims=True)
    return d * lax.rsqrt(var + LN_EPS) * g + b


def _silu(x):
    return x * jax.nn.sigmoid(x)


class _CastJob(NamedTuple):
    src: jax.Array
    col0: int
    cols: int
    transpose: bool = False


def _whole(w):
    return _CastJob(w, 0, w.shape[1])


def _cast_plan(jobs, n_steps, flat_step):
    in_specs, out_specs, out_shapes = [], [], []
    for job in jobs:
        rows = job.src.shape[0]
        align = LANES if job.transpose else BF16_SUBLANES
        chunk = align
        while rows % chunk or rows // chunk > n_steps:
            chunk += align
        last = rows // chunk - 1
        col_block = job.col0 // job.cols
        assert col_block * job.cols == job.col0

        def chunk_of(*g, last=last):
            return jnp.minimum(flat_step(*g), last)

        in_specs.append(pl.BlockSpec(
            (chunk, job.cols), lambda *g, f=chunk_of, cb=col_block: (f(*g), cb)))
        if job.transpose:
            out_specs.append(pl.BlockSpec((job.cols, chunk), lambda *g, f=chunk_of: (0, f(*g))))
            out_shapes.append(jax.ShapeDtypeStruct((job.cols, rows), BF16))
        else:
            out_specs.append(pl.BlockSpec((chunk, job.cols), lambda *g, f=chunk_of: (f(*g), 0)))
            out_shapes.append(jax.ShapeDtypeStruct((rows, job.cols), BF16))
    return in_specs, out_specs, out_shapes


def _run_cast_jobs(src_refs, dst_refs, transposes):
    for src_ref, dst_ref, transpose in zip(src_refs, dst_refs, transposes):
        w = src_ref[...]
        dst_ref[...] = (w.T if transpose else w).astype(BF16)


def _ada_kernel(*refs, transposes):
    n_cast = len(transposes)
    c_ref, w_ref, b_ref = refs[:3]
    o_ref = refs[3 + n_cast]
    _run_cast_jobs(refs[3:3 + n_cast], refs[4 + n_cast:], transposes)
    cond = _silu(c_ref[...]).astype(BF16)
    o_ref[0] = jnp.dot(cond, w_ref[...].astype(BF16), preferred_element_type=F32) + b_ref[0]


def _ada_mod(c, w_ada, b_ada, cast_jobs=()):
    bsz, d = c.shape
    cast_in, cast_out, cast_shapes = _cast_plan(cast_jobs, N_MOD, lambda j: j)
    outs = pl.pallas_call(
        functools.partial(_ada_kernel, transposes=tuple(j.transpose for j in cast_jobs)),
        grid=(N_MOD,),
        in_specs=[
            pl.BlockSpec((bsz, d), lambda j: (0, 0)),
            pl.BlockSpec((d, d), lambda j: (0, j)),
            pl.BlockSpec((1, 1, d), lambda j: (j, 0, 0)),
        ] + cast_in,
        out_specs=[pl.BlockSpec((1, bsz, d), lambda j: (j, 0, 0))] + cast_out,
        out_shape=[jax.ShapeDtypeStruct((N_MOD, bsz, d), F32)] + cast_shapes,
        compiler_params=pltpu.CompilerParams(
            dimension_semantics=("arbitrary",), vmem_limit_bytes=VMEM_LIMIT_BYTES),
        name="ada_mod",
    )(c, w_ada, b_ada.reshape(N_MOD, 1, d), *[j.src for j in cast_jobs])
    return outs[0], tuple(outs[1:])


def _ffn_kernel(*refs, transposes, mod_base, alpha):
    n_cast = len(transposes)
    x_ref, mod_ref, wgu_ref, wd_ref, g_ref, b_ref = refs[:6]
    o_ref = refs[6 + n_cast]
    scratch = refs[7 + 2 * n_cast:]
    _run_cast_jobs(refs[6:6 + n_cast], refs[7 + n_cast:7 + 2 * n_cast], transposes)

    d_ff = wd_ref.shape[0]
    tf = FFN_CHUNK
    n_sub = len(scratch) // 2
    batch = pl.ds(pl.program_id(0), 1)
    sh = mod_ref[mod_base, batch, :]
    sc = mod_ref[mod_base + 1, batch, :]
    gate = mod_ref[mod_base + 2, batch, :]
    h_ref, act_ref = scratch
    tm = h_ref.shape[0]
    h_ref[...] = (x_ref[0] * (1.0 + sc) + sh).astype(BF16)

    def gate_up(rows, c):
        lo = c * tf
        g = jnp.dot(h_ref[rows, :], wgu_ref[:, lo:lo + tf], preferred_element_type=F32)
        u = jnp.dot(h_ref[rows, :], wgu_ref[:, d_ff + lo:d_ff + lo + tf],
                    preferred_element_type=F32)
        act_ref[rows, lo:lo + tf] = (_silu(g) * u).astype(BF16)

    gate_up(slice(0, tm), 0)
    row0 = 0
    for sub in FFN_SUB_ROWS:
        rows = slice(row0, row0 + sub)
        row0 = rows.stop
        for c in range(1, d_ff // tf):
            gate_up(rows, c)
        pieces = DOWN_PIECES
        step = sub // pieces
        for p in range(pieces):
            part = slice(rows.start + p * step, rows.start + (p + 1) * step)
            down = jnp.dot(act_ref[part, :], wd_ref[...], preferred_element_type=F32)
            y = alpha * x_ref[0, part, :] + (0.5 * (1.0 + gate)) * down
            o_ref[0, part, :] = _layer_norm(y, g_ref[...], b_ref[...])


def _ffn_block(x, mod, w_gate_up, w_down, ln_g, ln_b, *, mod_base, alpha, cast_jobs=()):
    bsz, seq, d = x.shape
    tm = FFN_ROWS
    per_seq = seq // tm
    const2 = lambda b, s: (0, 0)
    cast_in, cast_out, cast_shapes = _cast_plan(
        cast_jobs, bsz * per_seq, lambda b, s: b * per_seq + s)
    outs = pl.pallas_call(
        functools.partial(_ffn_kernel, transposes=tuple(j.transpose for j in cast_jobs),
                          mod_base=mod_base, alpha=alpha),
        grid=(bsz, per_seq),
        in_specs=[
            pl.BlockSpec((1, tm, d), lambda b, s: (b, s, 0)),
            pl.BlockSpec((N_MOD, bsz, d), lambda b, s: (0, 0, 0)),
            pl.BlockSpec((d, 2 * D_FF), const2, pipeline_mode=pl.Buffered(1)),
            pl.BlockSpec((D_FF, d), const2, pipeline_mode=pl.Buffered(1)),
            pl.BlockSpec((1, d), const2),
            pl.BlockSpec((1, d), const2),
        ] + cast_in,
        out_specs=[pl.BlockSpec((1, tm, d), lambda b, s: (b, s, 0))] + cast_out,
        out_shape=[jax.ShapeDtypeStruct((bsz, seq, d), F32)] + cast_shapes,
        scratch_shapes=[pltpu.VMEM((tm, d), BF16), pltpu.VMEM((tm, D_FF), BF16)],
        compiler_params=pltpu.CompilerParams(
            dimension_semantics=("arbitrary", "arbitrary"),
            vmem_limit_bytes=VMEM_LIMIT_BYTES),
        name="ffn_block",
    )(x, mod, w_gate_up, w_down, ln_g.reshape(1, d), ln_b.reshape(1, d),
      *[j.src for j in cast_jobs])
    return outs[0], tuple(outs[1:])


def _mixer_kernel(x_ref, mod_ref, pos_ref, invf_ref, sink_ref, wqkv_ref, wconv_a_ref, wconv_b_ref,
                  cw_ref, wout_ref, g_ref, b_ref, bias_ref, o_ref,
                  k_ref, vt_ref, z_ref, hall_ref, ptall_ref, *sub_scratch, alpha):
    ts = x_ref.shape[1]
    n_sub = len(sub_scratch) // MIX_SUB_BUFFERS
    sub = ts // n_sub
    s_idx = pl.program_id(1)
    zpad = SUBLANES
    n_slabs = CONV_WIDTH // LANES

    @pl.when(s_idx == 0)
    def _():
        k_ref[0:WINDOW, :] = jnp.zeros((WINDOW, KV_WIDTH), BF16)
        vt_ref[:, 0:WINDOW] = jnp.zeros((KV_WIDTH, WINDOW), BF16)
        z_ref[:, 0:zpad, :] = jnp.zeros((n_slabs, zpad, LANES), F32)

    batch = pl.ds(pl.program_id(0), 1)
    sh = mod_ref[3, batch, :]
    sc = mod_ref[4, batch, :]
    gate = mod_ref[5, batch, :]
    contract_last = (((1,), (1,)), ((), ()))
    half = ROT_DIM // 2
    scale = HEAD_DIM ** -0.5 * LOG2_E

    rows2 = 2 * WINDOW
    no_prev = jnp.where(s_idx == 0, WINDOW, 0)
    first_pen = jnp.where(lax.broadcasted_iota(jnp.int32, (rows2, 1), 0) < no_prev, NEG_BIG, 0.0)
    sink = sink_ref[...] * LOG2_E

    hall_ref[...] = (x_ref[0] * (1.0 + sc) + sh).astype(BF16)
    ptall_ref[...] = lax.dot_general(wqkv_ref[...], hall_ref[...], contract_last,
                                     preferred_element_type=F32)

    for r in range(n_sub):
        qbd_ref, at_ref, pc_ref = sub_scratch[
            r * MIX_SUB_BUFFERS:(r + 1) * MIX_SUB_BUFFERS]
        t0 = r * sub
        x = x_ref[0, t0:t0 + sub, :]
        h_ref = hall_ref.at[t0:t0 + sub, :]
        pt_ref = ptall_ref.at[:, t0:t0 + sub]

        ang = invf_ref[...] * pos_ref[0, :, t0:t0 + sub].astype(F32)
        cos_t = jnp.cos(ang)
        sin_t = jnp.sin(ang)

        def rope(head):
            x1 = head[0:half]
            x2 = head[half:ROT_DIM]
            return jnp.concatenate(
                [x1 * cos_t - x2 * sin_t, x2 * cos_t + x1 * sin_t, head[ROT_DIM:]], axis=0)

        qbd_ref[...] = jnp.zeros(qbd_ref.shape, BF16)
        for hd in range(N_Q_HEADS):
            kv = hd // (N_Q_HEADS // N_KV_HEADS)
            qh = (rope(pt_ref[hd * HEAD_DIM:(hd + 1) * HEAD_DIM, :]) * scale).astype(BF16)
            for n in range(sub // WINDOW):
                qbd_ref[n, kv * HEAD_DIM:(kv + 1) * HEAD_DIM, hd * WINDOW:(hd + 1) * WINDOW] = (
                    qh[:, n * WINDOW:(n + 1) * WINDOW])
        kt = jnp.concatenate(
            [rope(pt_ref[ATTN_WIDTH + kv * HEAD_DIM:ATTN_WIDTH + (kv + 1) * HEAD_DIM, :])
             for kv in range(N_KV_HEADS)], axis=0)
        k_ref[WINDOW + t0:WINDOW + t0 + sub, :] = kt.T.astype(BF16)
        vt_ref[:, WINDOW + t0:WINDOW + t0 + sub] = pt_ref[
            ATTN_WIDTH + KV_WIDTH:ATTN_WIDTH + 2 * KV_WIDTH, :].astype(BF16)

        def conv_proj(c, h_ref=h_ref, pc_ref=pc_ref):
            per_half = wconv_a_ref.shape[1] // MIX_CONV_CHUNK
            w_ref = wconv_a_ref if c < per_half else wconv_b_ref
            src = slice((c % per_half) * MIX_CONV_CHUNK, (c % per_half + 1) * MIX_CONV_CHUNK)
            dst = slice(c * MIX_CONV_CHUNK, (c + 1) * MIX_CONV_CHUNK)
            pc_ref[:, dst] = jnp.dot(h_ref[...], w_ref[:, src], preferred_element_type=F32)

        def scores(n, qbd_ref=qbd_ref, t0=t0):
            r0 = t0 + n * WINDOW
            return jnp.dot(k_ref[r0:r0 + rows2, :], qbd_ref[n], preferred_element_type=F32)

        def attend(n, sco, at_ref=at_ref, t0=t0):
            r0 = t0 + n * WINDOW
            sco = sco + bias_ref[...]
            if r0 == 0:
                sco = sco + first_pen
            mx = jnp.maximum(jnp.max(sco, axis=0, keepdims=True), sink)
            pr = jnp.exp2(sco - mx)
            den = jnp.sum(pr, axis=0, keepdims=True) + jnp.exp2(sink - mx)
            pv = jnp.dot(vt_ref[:, r0:r0 + rows2], pr.astype(BF16), preferred_element_type=F32)
            pv = pv * (1.0 / den)
            for pair in range(N_Q_HEADS // 2):
                kv = 2 * pair // (N_Q_HEADS // N_KV_HEADS)
                feat = pv[kv * HEAD_DIM:(kv + 1) * HEAD_DIM, :]
                blk = jnp.concatenate(
                    [feat[:, (2 * pair) * WINDOW:(2 * pair + 1) * WINDOW],
                     feat[:, (2 * pair + 1) * WINDOW:(2 * pair + 2) * WINDOW]], axis=0)
                at_ref[n * WINDOW:(n + 1) * WINDOW, pair * LANES:(pair + 1) * LANES] = (
                    blk.T.astype(BF16))

        n_blocks = sub // WINDOW
        n_chunks = 3 * CONV_WIDTH // MIX_CONV_CHUNK
        chunk = 0
        for n in range(n_blocks):
            pending = scores(n)
            conv_proj(chunk)
            chunk += 1
            attend(n, pending)
        while chunk < n_chunks:
            conv_proj(chunk)
            chunk += 1
        pieces = LAST_PIECES if r == n_sub - 1 else 1
        part_rows = sub // pieces

        z0 = zpad + t0
        for j in range(n_slabs):
            lanes = slice(j * LANES, (j + 1) * LANES)
            u = pc_ref[:, j * LANES:(j + 1) * LANES]
            b_gate = pc_ref[:, CONV_WIDTH + j * LANES:CONV_WIDTH + (j + 1) * LANES]
            c_gate = pc_ref[:, 2 * CONV_WIDTH + j * LANES:2 * CONV_WIDTH + (j + 1) * LANES]
            z_ref[j, z0:z0 + sub, :] = c_gate * u
            y = cw_ref[0:1, lanes] * z_ref[j, z0 - 2:z0 - 2 + sub, :]
            y = y + cw_ref[1:2, lanes] * z_ref[j, z0 - 1:z0 - 1 + sub, :]
            y = y + cw_ref[2:3, lanes] * z_ref[j, z0:z0 + sub, :]
            at_ref[:, ATTN_WIDTH + j * LANES:ATTN_WIDTH + (j + 1) * LANES] = (
                b_gate * y).astype(BF16)

        for p in range(pieces):
            part = slice(p * part_rows, (p + 1) * part_rows)
            mix = jnp.dot(at_ref[part, :], wout_ref[...], preferred_element_type=F32)
            o_ref[0, t0 + part.start:t0 + part.stop, :] = _layer_norm(
                alpha * x[part] + (1.0 + gate) * mix, g_ref[...], b_ref[...])

    k_ref[0:WINDOW, :] = k_ref[ts:ts + WINDOW, :]
    vt_ref[:, 0:WINDOW] = vt_ref[:, ts:ts + WINDOW]
    z_ref[:, 0:zpad, :] = z_ref[:, ts:ts + zpad, :]


def _mixer_block(x, mod, positions, w_qkv_t, w_conv_a, w_conv_b, conv_w, sinks, w_out, ln_g, ln_b,
                 *, alpha):
    bsz, seq, d = x.shape
    ts = MIX_ROWS
    sub = MIX_SUB_ROWS
    qkv = ATTN_WIDTH + 2 * KV_WIDTH
    conv_half = 3 * CONV_WIDTH // 2
    inv_freq = jnp.power(jnp.float32(ROPE_THETA),
                         -jnp.arange(0, ROT_DIM, 2, dtype=F32) / ROT_DIM)
    sink_row = jnp.repeat(sinks.astype(F32), WINDOW).reshape(1, N_Q_HEADS * WINDOW)
    ki = lax.broadcasted_iota(jnp.int32, (2 * WINDOW, N_Q_HEADS * WINDOW), 0)
    qi = lax.broadcasted_iota(jnp.int32, (2 * WINDOW, N_Q_HEADS * WINDOW), 1) % WINDOW
    band_bias = jnp.where((ki > qi) & (ki <= qi + WINDOW), 0.0, NEG_BIG).astype(F32)
    const2 = lambda b, s: (0, 0)
    sub_scratch = [
        pltpu.VMEM((sub // WINDOW, KV_WIDTH, N_Q_HEADS * WINDOW), BF16),
        pltpu.VMEM((sub, d), BF16),
        pltpu.VMEM((sub, 3 * CONV_WIDTH), F32),
    ]
    assert len(sub_scratch) == MIX_SUB_BUFFERS
    return pl.pallas_call(
        functools.partial(_mixer_kernel, alpha=alpha),
        grid=(bsz, seq // ts),
        in_specs=[
            pl.BlockSpec((1, ts, d), lambda b, s: (b, s, 0)),
            pl.BlockSpec((N_MOD, bsz, d), lambda b, s: (0, 0, 0)),
            pl.BlockSpec((1, 1, ts), lambda b, s: (b, 0, s)),
            pl.BlockSpec((ROT_DIM // 2, 1), const2),
            pl.BlockSpec((1, N_Q_HEADS * WINDOW), const2),
            pl.BlockSpec((qkv, d), const2, pipeline_mode=pl.Buffered(1)),
            pl.BlockSpec((d, conv_half), const2, pipeline_mode=pl.Buffered(1)),
            pl.BlockSpec((d, conv_half), const2, pipeline_mode=pl.Buffered(1)),
            pl.BlockSpec((CONV_W, CONV_WIDTH), const2),
            pl.BlockSpec((d, d), const2, pipeline_mode=pl.Buffered(1)),
            pl.BlockSpec((1, d), const2),
            pl.BlockSpec((1, d), const2),
            pl.BlockSpec((2 * WINDOW, N_Q_HEADS * WINDOW), const2, pipeline_mode=pl.Buffered(1)),
        ],
        out_specs=pl.BlockSpec((1, ts, d), lambda b, s: (b, s, 0)),
        out_shape=jax.ShapeDtypeStruct((bsz, seq, d), F32),
        scratch_shapes=[
            pltpu.VMEM((ts + WINDOW, KV_WIDTH), BF16),
            pltpu.VMEM((KV_WIDTH, ts + WINDOW), BF16),
            pltpu.VMEM((CONV_WIDTH // LANES, ts + SUBLANES, LANES), F32),
            pltpu.VMEM((ts, d), BF16),
            pltpu.VMEM((qkv, ts), F32),
        ] + sub_scratch * (ts // sub),
        compiler_params=pltpu.CompilerParams(
            dimension_semantics=("arbitrary", "arbitrary"),
            vmem_limit_bytes=VMEM_LIMIT_BYTES),
        name="mixer_block",
    )(x, mod, positions.reshape(bsz, 1, seq), inv_freq.reshape(ROT_DIM // 2, 1), sink_row,
      w_qkv_t, w_conv_a, w_conv_b, conv_w, w_out, ln_g.reshape(1, d), ln_b.reshape(1, d),
      band_bias)


def kernel(x, c, positions, w_ada, b_ada, ffn1_w_gate_up, ffn1_w_down, ln1_g, ln1_b, w_in, conv_w, attn_sinks, w_out, ln2_g, ln2_b, ffn2_w_gate_up, ffn2_w_down, ln3_g, ln3_b):
    depth = w_ada.shape[0]
    alpha = (2.0 * depth) ** 0.25
    qkv = ATTN_WIDTH + 2 * KV_WIDTH
    ffn1_w = None
    for l in range(depth):
        first = () if ffn1_w else (_whole(ffn1_w_gate_up[l]), _whole(ffn1_w_down[l]))
        mod, cast = _ada_mod(c, w_ada[l], b_ada[l], cast_jobs=first)
        ffn1_w = ffn1_w or cast
        x, (w_qkv_t, w_conv_a, w_conv_b, w_out_b, ffn2_wgu, ffn2_wd) = _ffn_block(
            x, mod, *ffn1_w, ln1_g[l], ln1_b[l], mod_base=0, alpha=alpha,
            cast_jobs=(_CastJob(w_in[l], 0, qkv, transpose=True),
                       _CastJob(w_in[l], qkv, qkv), _CastJob(w_in[l], 2 * qkv, qkv),
                       _whole(w_out[l]), _whole(ffn2_w_gate_up[l]), _whole(ffn2_w_down[l])))
        x = _mixer_block(x, mod, positions, w_qkv_t, w_conv_a, w_conv_b, conv_w[l], attn_sinks[l],
                         w_out_b, ln2_g[l], ln2_b[l], alpha=alpha)
        next_ffn1 = ((_whole(ffn1_w_gate_up[l + 1]), _whole(ffn1_w_down[l + 1]))
                     if l + 1 < depth else ())
        x, ffn1_w = _ffn_block(x, mod, ffn2_wgu, ffn2_wd, ln3_g[l], ln3_b[l],
                               mod_base=6, alpha=alpha, cast_jobs=next_ffn1)
    return x
```

```python
import functools
from typing import NamedTuple

import jax
import jax.numpy as jnp
from jax import lax
from jax.experimental import pallas as pl
from jax.experimental.pallas import tpu as pltpu

HEAD_DIM = 64
ATTN_WIDTH = 512
CONV_WIDTH = 512
N_Q_HEADS = 8
N_KV_HEADS = 2
KV_WIDTH = N_KV_HEADS * HEAD_DIM
WINDOW = 128
ROT_DIM = 16
ROPE_THETA = 500000.0
CONV_W = 3
D_FF = 2816
N_MOD = 9
LN_EPS = 1e-5

LANES = 128
SUBLANES = 8
BF16_SUBLANES = 16
VMEM_LIMIT_BYTES = 56 * 1024 * 1024

FFN_ROWS = 1024
FFN_SUB_ROWS = (512, 512)
FFN_CHUNK = 256
LAST_PIECES = 2
DOWN_PIECES = 2
MIX_ROWS = 1024
MIX_SUB_ROWS = 512
MIX_SUB_BUFFERS = 3
MIX_CONV_CHUNK = 256
NEG_BIG = -1e30
LOG2_E = 1.4426950408889634

F32 = jnp.float32
BF16 = jnp.bfloat16


def _layer_norm(y, g, b):
    mu = jnp.mean(y, axis=-1, keepdims=True)
    d = y - mu
    var = jnp.mean(d * d, axis=-1, keepdims=True)
    return d * lax.rsqrt(var + LN_EPS) * g + b


def _silu(x):
    return x * jax.nn.sigmoid(x)


class _CastJob(NamedTuple):
    src: jax.Array
    col0: int
    cols: int
    transpose: bool = False


def _whole(w):
    return _CastJob(w, 0, w.shape[1])


def _cast_plan(jobs, n_steps, flat_step):
    in_specs, out_specs, out_shapes = [], [], []
    for job in jobs:
        rows = job.src.shape[0]
        align = LANES if job.transpose else BF16_SUBLANES
        chunk = align
        while rows % chunk or rows // chunk > n_steps:
            chunk += align
        last = rows // chunk - 1
        col_block = job.col0 // job.cols
        assert col_block * job.cols == job.col0

        def chunk_of(*g, last=last):
            return jnp.minimum(flat_step(*g), last)

        in_specs.append(pl.BlockSpec(
            (chunk, job.cols), lambda *g, f=chunk_of, cb=col_block: (f(*g), cb)))
        if job.transpose:
            out_specs.append(pl.BlockSpec((job.cols, chunk), lambda *g, f=chunk_of: (0, f(*g))))
            out_shapes.append(jax.ShapeDtypeStruct((job.cols, rows), BF16))
        else:
            out_specs.append(pl.BlockSpec((chunk, job.cols), lambda *g, f=chunk_of: (f(*g), 0)))
            out_shapes.append(jax.ShapeDtypeStruct((rows, job.cols), BF16))
    return in_specs, out_specs, out_shapes


def _run_cast_jobs(src_refs, dst_refs, transposes):
    for src_ref, dst_ref, transpose in zip(src_refs, dst_refs, transposes):
        w = src_ref[...]
        dst_ref[...] = (w.T if transpose else w).astype(BF16)


def _ada_kernel(*refs, transposes):
    n_cast = len(transposes)
    c_ref, w_ref, b_ref = refs[:3]
    o_ref = refs[3 + n_cast]
    _run_cast_jobs(refs[3:3 + n_cast], refs[4 + n_cast:], transposes)
    cond = _silu(c_ref[...]).astype(BF16)
    o_ref[0] = jnp.dot(cond, w_ref[...].astype(BF16), preferred_element_type=F32) + b_ref[0]


def _ada_mod(c, w_ada, b_ada, cast_jobs=()):
    bsz, d = c.shape
    cast_in, cast_out, cast_shapes = _cast_plan(cast_jobs, N_MOD, lambda j: j)
    outs = pl.pallas_call(
        functools.partial(_ada_kernel, transposes=tuple(j.transpose for j in cast_jobs)),
        grid=(N_MOD,),
        in_specs=[
            pl.BlockSpec((bsz, d), lambda j: (0, 0)),
            pl.BlockSpec((d, d), lambda j: (0, j)),
            pl.BlockSpec((1, 1, d), lambda j: (j, 0, 0)),
        ] + cast_in,
        out_specs=[pl.BlockSpec((1, bsz, d), lambda j: (j, 0, 0))] + cast_out,
        out_shape=[jax.ShapeDtypeStruct((N_MOD, bsz, d), F32)] + cast_shapes,
        compiler_params=pltpu.CompilerParams(
            dimension_semantics=("arbitrary",), vmem_limit_bytes=VMEM_LIMIT_BYTES),
        name="ada_mod",
    )(c, w_ada, b_ada.reshape(N_MOD, 1, d), *[j.src for j in cast_jobs])
    return outs[0], tuple(outs[1:])


def _ffn_kernel(*refs, transposes, mod_base, alpha):
    n_cast = len(transposes)
    x_ref, mod_ref, wgu_ref, wd_ref, g_ref, b_ref = refs[:6]
    o_ref = refs[6 + n_cast]
    scratch = refs[7 + 2 * n_cast:]
    _run_cast_jobs(refs[6:6 + n_cast], refs[7 + n_cast:7 + 2 * n_cast], transposes)

    d_ff = wd_ref.shape[0]
    tf = FFN_CHUNK
    batch = pl.ds(pl.program_id(0), 1)
    sh = mod_ref[mod_base, batch, :]
    sc = mod_ref[mod_base + 1, batch, :]
    gate = mod_ref[mod_base + 2, batch, :]
    h_ref, act_ref = scratch
    tm = h_ref.shape[0]
    h_ref[...] = (x_ref[0] * (1.0 + sc) + sh).astype(BF16)

    def gate_up(rows, c):
        lo = c * tf
        g = jnp.dot(h_ref[rows, :], wgu_ref[:, lo:lo + tf], preferred_element_type=F32)
        u = jnp.dot(h_ref[rows, :], wgu_ref[:, d_ff + lo:d_ff + lo + tf],
                    preferred_element_type=F32)
        act_ref[rows, lo:lo + tf] = (_silu(g) * u).astype(BF16)

    gate_up(slice(0, tm), 0)
    row0 = 0
    for sub in FFN_SUB_ROWS:
        rows = slice(row0, row0 + sub)
        row0 = rows.stop
        for c in range(1, d_ff // tf):
            gate_up(rows, c)
        pieces = DOWN_PIECES
        step = sub // pieces
        for p in range(pieces):
            part = slice(rows.start + p * step, rows.start + (p + 1) * step)
            down = jnp.dot(act_ref[part, :], wd_ref[...], preferred_element_type=F32)
            y = alpha * x_ref[0, part, :] + (0.5 * (1.0 + gate)) * down
            o_ref[0, part, :] = _layer_norm(y, g_ref[...], b_ref[...])


def _ffn_block(x, mod, w_gate_up, w_down, ln_g, ln_b, *, mod_base, alpha, cast_jobs=()):
    bsz, seq, d = x.shape
    tm = FFN_ROWS
    per_seq = seq // tm
    const2 = lambda b, s: (0, 0)
    cast_in, cast_out, cast_shapes = _cast_plan(
        cast_jobs, bsz * per_seq, lambda b, s: b * per_seq + s)
    outs = pl.pallas_call(
        functools.partial(_ffn_kernel, transposes=tuple(j.transpose for j in cast_jobs),
                          mod_base=mod_base, alpha=alpha),
        grid=(bsz, per_seq),
        in_specs=[
            pl.BlockSpec((1, tm, d), lambda b, s: (b, s, 0)),
            pl.BlockSpec((N_MOD, bsz, d), lambda b, s: (0, 0, 0)),
            pl.BlockSpec((d, 2 * D_FF), const2, pipeline_mode=pl.Buffered(1)),
            pl.BlockSpec((D_FF, d), const2, pipeline_mode=pl.Buffered(1)),
            pl.BlockSpec((1, d), const2),
            pl.BlockSpec((1, d), const2),
        ] + cast_in,
        out_specs=[pl.BlockSpec((1, tm, d), lambda b, s: (b, s, 0))] + cast_out,
        out_shape=[jax.ShapeDtypeStruct((bsz, seq, d), F32)] + cast_shapes,
        scratch_shapes=[pltpu.VMEM((tm, d), BF16), pltpu.VMEM((tm, D_FF), BF16)],
        compiler_params=pltpu.CompilerParams(
            dimension_semantics=("arbitrary", "arbitrary"),
            vmem_limit_bytes=VMEM_LIMIT_BYTES),
        name="ffn_block",
    )(x, mod, w_gate_up, w_down, ln_g.reshape(1, d), ln_b.reshape(1, d),
      *[j.src for j in cast_jobs])
    return outs[0], tuple(outs[1:])


def _mixer_kernel(x_ref, mod_ref, pos_ref, invf_ref, sink_ref, wqkv_ref, wconv_a_ref, wconv_b_ref,
                  cw_ref, wout_ref, g_ref, b_ref, bias_ref, o_ref,
                  k_ref, vt_ref, z_ref, hall_ref, ptall_ref, *sub_scratch, alpha):
    ts = x_ref.shape[1]
    n_sub = len(sub_scratch) // MIX_SUB_BUFFERS
    sub = ts // n_sub
    s_idx = pl.program_id(1)
    zpad = SUBLANES
    n_slabs = CONV_WIDTH // LANES

    @pl.when(s_idx == 0)
    def _():
        k_ref[0:WINDOW, :] = jnp.zeros((WINDOW, KV_WIDTH), BF16)
        vt_ref[:, 0:WINDOW] = jnp.zeros((KV_WIDTH, WINDOW), BF16)
        z_ref[:, 0:zpad, :] = jnp.zeros((n_slabs, zpad, LANES), F32)

    batch = pl.ds(pl.program_id(0), 1)
    sh = mod_ref[3, batch, :]
    sc = mod_ref[4, batch, :]
    gate = mod_ref[5, batch, :]
    contract_last = (((1,), (1,)), ((), ()))
    half = ROT_DIM // 2
    scale = HEAD_DIM ** -0.5 * LOG2_E

    rows2 = 2 * WINDOW
    no_prev = jnp.where(s_idx == 0, WINDOW, 0)
    first_pen = jnp.where(lax.broadcasted_iota(jnp.int32, (rows2, 1), 0) < no_prev, NEG_BIG, 0.0)
    sink = sink_ref[...] * LOG2_E

    hall_ref[...] = (x_ref[0] * (1.0 + sc) + sh).astype(BF16)
    ptall_ref[...] = lax.dot_general(wqkv_ref[...], hall_ref[...], contract_last,
                                     preferred_element_type=F32)

    for r in range(n_sub):
        qbd_ref, at_ref, pc_ref = sub_scratch[
            r * MIX_SUB_BUFFERS:(r + 1) * MIX_SUB_BUFFERS]
        t0 = r * sub
        x = x_ref[0, t0:t0 + sub, :]
        h_ref = hall_ref.at[t0:t0 + sub, :]
        pt_ref = ptall_ref.at[:, t0:t0 + sub]

        ang = invf_ref[...] * pos_ref[batch, t0:t0 + sub].astype(F32)
        cos_t = jnp.cos(ang)
        sin_t = jnp.sin(ang)

        def rope(head):
            x1 = head[0:half]
            x2 = head[half:ROT_DIM]
            return jnp.concatenate(
                [x1 * cos_t - x2 * sin_t, x2 * cos_t + x1 * sin_t, head[ROT_DIM:]], axis=0)

        qbd_ref[...] = jnp.zeros(qbd_ref.shape, BF16)
        for hd in range(N_Q_HEADS):
            kv = hd // (N_Q_HEADS // N_KV_HEADS)
            qh = (rope(pt_ref[hd * HEAD_DIM:(hd + 1) * HEAD_DIM, :]) * scale).astype(BF16)
            for n in range(sub // WINDOW):
                qbd_ref[n, kv * HEAD_DIM:(kv + 1) * HEAD_DIM, hd * WINDOW:(hd + 1) * WINDOW] = (
                    qh[:, n * WINDOW:(n + 1) * WINDOW])
        kt = jnp.concatenate(
            [rope(pt_ref[ATTN_WIDTH + kv * HEAD_DIM:ATTN_WIDTH + (kv + 1) * HEAD_DIM, :])
             for kv in range(N_KV_HEADS)], axis=0)
        k_ref[WINDOW + t0:WINDOW + t0 + sub, :] = kt.T.astype(BF16)
        vt_ref[:, WINDOW + t0:WINDOW + t0 + sub] = pt_ref[
            ATTN_WIDTH + KV_WIDTH:ATTN_WIDTH + 2 * KV_WIDTH, :].astype(BF16)

        def conv_proj(c, h_ref=h_ref, pc_ref=pc_ref):
            per_half = wconv_a_ref.shape[1] // MIX_CONV_CHUNK
            w_ref = wconv_a_ref if c < per_half else wconv_b_ref
            src = slice((c % per_half) * MIX_CONV_CHUNK, (c % per_half + 1) * MIX_CONV_CHUNK)
            dst = slice(c * MIX_CONV_CHUNK, (c + 1) * MIX_CONV_CHUNK)
            pc_ref[:, dst] = jnp.dot(h_ref[...], w_ref[:, src], preferred_element_type=F32)

        def scores(n, qbd_ref=qbd_ref, t0=t0):
            r0 = t0 + n * WINDOW
            return jnp.dot(k_ref[r0:r0 + rows2, :], qbd_ref[n], preferred_element_type=F32)

        def attend(n, sco, at_ref=at_ref, t0=t0):
            r0 = t0 + n * WINDOW
            sco = sco + bias_ref[...]
            if r0 == 0:
                sco = sco + first_pen
            mx = jnp.maximum(jnp.max(sco, axis=0, keepdims=True), sink)
            pr = jnp.exp2(sco - mx)
            den = jnp.sum(pr, axis=0, keepdims=True) + jnp.exp2(sink - mx)
            pv = jnp.dot(vt_ref[:, r0:r0 + rows2], pr.astype(BF16), preferred_element_type=F32)
            pv = pv * (1.0 / den)
            for pair in range(N_Q_HEADS // 2):
                kv = 2 * pair // (N_Q_HEADS // N_KV_HEADS)
                feat = pv[kv * HEAD_DIM:(kv + 1) * HEAD_DIM, :]
                blk = jnp.concatenate(
                    [feat[:, (2 * pair) * WINDOW:(2 * pair + 1) * WINDOW],
                     feat[:, (2 * pair + 1) * WINDOW:(2 * pair + 2) * WINDOW]], axis=0)
                at_ref[n * WINDOW:(n + 1) * WINDOW, pair * LANES:(pair + 1) * LANES] = (
                    blk.T.astype(BF16))

        n_blocks = sub // WINDOW
        n_chunks = 3 * CONV_WIDTH // MIX_CONV_CHUNK
        chunk = 0
        for n in range(n_blocks):
            pending = scores(n)
            conv_proj(chunk)
            chunk += 1
            attend(n, pending)
        while chunk < n_chunks:
            conv_proj(chunk)
            chunk += 1
        pieces = LAST_PIECES if r == n_sub - 1 else 1
        part_rows = sub // pieces

        z0 = zpad + t0
        for j in range(n_slabs):
            lanes = slice(j * LANES, (j + 1) * LANES)
            u = pc_ref[:, j * LANES:(j + 1) * LANES]
            b_gate = pc_ref[:, CONV_WIDTH + j * LANES:CONV_WIDTH + (j + 1) * LANES]
            c_gate = pc_ref[:, 2 * CONV_WIDTH + j * LANES:2 * CONV_WIDTH + (j + 1) * LANES]
            z_ref[j, z0:z0 + sub, :] = c_gate * u
            y = cw_ref[0:1, lanes] * z_ref[j, z0 - 2:z0 - 2 + sub, :]
            y = y + cw_ref[1:2, lanes] * z_ref[j, z0 - 1:z0 - 1 + sub, :]
            y = y + cw_ref[2:3, lanes] * z_ref[j, z0:z0 + sub, :]
            at_ref[:, ATTN_WIDTH + j * LANES:ATTN_WIDTH + (j + 1) * LANES] = (
                b_gate * y).astype(BF16)

        for p in range(pieces):
            part = slice(p * part_rows, (p + 1) * part_rows)
            mix = jnp.dot(at_ref[part, :], wout_ref[...], preferred_element_type=F32)
            o_ref[0, t0 + part.start:t0 + part.stop, :] = _layer_norm(
                alpha * x[part] + (1.0 + gate) * mix, g_ref[...], b_ref[...])

    k_ref[0:WINDOW, :] = k_ref[ts:ts + WINDOW, :]
    vt_ref[:, 0:WINDOW] = vt_ref[:, ts:ts + WINDOW]
    z_ref[:, 0:zpad, :] = z_ref[:, ts:ts + zpad, :]


def _mixer_block(x, mod, positions, w_qkv_t, w_conv_a, w_conv_b, conv_w, sinks, w_out, ln_g, ln_b,
                 *, alpha):
    bsz, seq, d = x.shape
    ts = MIX_ROWS
    sub = MIX_SUB_ROWS
    qkv = ATTN_WIDTH + 2 * KV_WIDTH
    conv_half = 3 * CONV_WIDTH // 2
    inv_freq = jnp.power(jnp.float32(ROPE_THETA),
                         -jnp.arange(0, ROT_DIM, 2, dtype=F32) / ROT_DIM)
    sink_row = jnp.repeat(sinks.astype(F32), WINDOW).reshape(1, N_Q_HEADS * WINDOW)
    ki = lax.broadcasted_iota(jnp.int32, (2 * WINDOW, N_Q_HEADS * WINDOW), 0)
    qi = lax.broadcasted_iota(jnp.int32, (2 * WINDOW, N_Q_HEADS * WINDOW), 1) % WINDOW
    band_bias = jnp.where((ki > qi) & (ki <= qi + WINDOW), 0.0, NEG_BIG).astype(F32)
    const2 = lambda b, s: (0, 0)
    sub_scratch = [
        pltpu.VMEM((sub // WINDOW, KV_WIDTH, N_Q_HEADS * WINDOW), BF16),
        pltpu.VMEM((sub, d), BF16),
        pltpu.VMEM((sub, 3 * CONV_WIDTH), F32),
    ]
    assert len(sub_scratch) == MIX_SUB_BUFFERS
    return pl.pallas_call(
        functools.partial(_mixer_kernel, alpha=alpha),
        grid=(bsz, seq // ts),
        in_specs=[
            pl.BlockSpec((1, ts, d), lambda b, s: (b, s, 0)),
            pl.BlockSpec((N_MOD, bsz, d), lambda b, s: (0, 0, 0)),
            pl.BlockSpec((bsz, ts), lambda b, s: (0, s)),
            pl.BlockSpec((ROT_DIM // 2, 1), const2),
            pl.BlockSpec((1, N_Q_HEADS * WINDOW), const2),
            pl.BlockSpec((qkv, d), const2, pipeline_mode=pl.Buffered(1)),
            pl.BlockSpec((d, conv_half), const2, pipeline_mode=pl.Buffered(1)),
            pl.BlockSpec((d, conv_half), const2, pipeline_mode=pl.Buffered(1)),
            pl.BlockSpec((CONV_W, CONV_WIDTH), const2),
            pl.BlockSpec((d, d), const2, pipeline_mode=pl.Buffered(1)),
            pl.BlockSpec((1, d), const2),
            pl.BlockSpec((1, d), const2),
            pl.BlockSpec((2 * WINDOW, N_Q_HEADS * WINDOW), const2, pipeline_mode=pl.Buffered(1)),
        ],
        out_specs=pl.BlockSpec((1, ts, d), lambda b, s: (b, s, 0)),
        out_shape=jax.ShapeDtypeStruct((bsz, seq, d), F32),
        scratch_shapes=[
            pltpu.VMEM((ts + WINDOW, KV_WIDTH), BF16),
            pltpu.VMEM((KV_WIDTH, ts + WINDOW), BF16),
            pltpu.VMEM((CONV_WIDTH // LANES, ts + SUBLANES, LANES), F32),
            pltpu.VMEM((ts, d), BF16),
            pltpu.VMEM((qkv, ts), F32),
        ] + sub_scratch * (ts // sub),
        compiler_params=pltpu.CompilerParams(
            dimension_semantics=("arbitrary", "arbitrary"),
            vmem_limit_bytes=VMEM_LIMIT_BYTES),
        name="mixer_block",
    )(x, mod, positions, inv_freq.reshape(ROT_DIM // 2, 1), sink_row,
      w_qkv_t, w_conv_a, w_conv_b, conv_w, w_out, ln_g.reshape(1, d), ln_b.reshape(1, d),
      band_bias)


def kernel(x, c, positions, w_ada, b_ada, ffn1_w_gate_up, ffn1_w_down, ln1_g, ln1_b, w_in, conv_w, attn_sinks, w_out, ln2_g, ln2_b, ffn2_w_gate_up, ffn2_w_down, ln3_g, ln3_b):
    depth = w_ada.shape[0]
    alpha = (2.0 * depth) ** 0.25
    qkv = ATTN_WIDTH + 2 * KV_WIDTH
    ffn1_w = None
    for l in range(depth):
        first = () if ffn1_w else (_whole(ffn1_w_gate_up[l]), _whole(ffn1_w_down[l]))
        mod, cast = _ada_mod(c, w_ada[l], b_ada[l], cast_jobs=first)
        ffn1_w = ffn1_w or cast
        x, (w_qkv_t, w_conv_a, w_conv_b, w_out_b, ffn2_wgu, ffn2_wd) = _ffn_block(
            x, mod, *ffn1_w, ln1_g[l], ln1_b[l], mod_base=0, alpha=alpha,
            cast_jobs=(_CastJob(w_in[l], 0, qkv, transpose=True),
                       _CastJob(w_in[l], qkv, qkv), _CastJob(w_in[l], 2 * qkv, qkv),
                       _whole(w_out[l]), _whole(ffn2_w_gate_up[l]), _whole(ffn2_w_down[l])))
        x = _mixer_block(x, mod, positions, w_qkv_t, w_conv_a, w_conv_b, conv_w[l], attn_sinks[l],
                         w_out_b, ln2_g[l], ln2_b[l], alpha=alpha)
        next_ffn1 = ((_whole(ffn1_w_gate_up[l + 1]), _whole(ffn1_w_down[l + 1]))
                     if l + 1 < depth else ())
        x, ffn1_w = _ffn_block(x, mod, ffn2_wgu, ffn2_wd, ln3_g[l], ln3_b[l],
                               mod_base=6, alpha=alpha, cast_jobs=next_ffn1)
    return x
```

```python
import functools
from typing import NamedTuple

import jax
import jax.numpy as jnp
import numpy as np
from jax import lax
from jax.experimental import pallas as pl
from jax.experimental.pallas import tpu as pltpu

D_MODEL = 1024
HEAD_DIM = 64
ATTN_WIDTH = 512
CONV_WIDTH = 512
N_Q_HEADS = 8
N_KV_HEADS = 2
KV_WIDTH = N_KV_HEADS * HEAD_DIM
WINDOW = 128
ROT_DIM = 16
ROPE_THETA = 500000.0
CONV_W = 3
D_FF = 2816
N_MOD = 9
LN_EPS = 1e-5
IN_WIDTH = ATTN_WIDTH + 2 * KV_WIDTH + 3 * CONV_WIDTH

LANES = 128
SUBLANES = 8
BF16_SUBLANES = 16
VMEM_LIMIT_BYTES = 56 * 1024 * 1024

FFN_ROWS = 1024
FFN_SUB_ROWS = (512, 512)
FFN_CHUNK = 256
LAST_PIECES = 2
DOWN_PIECES = 2
MIX_ROWS = 1024
MIX_SUB_ROWS = 512
MIX_SUB_BUFFERS = 3
MIX_CONV_CHUNK = 256
NEG_BIG = -1e30
LOG2_E = 1.4426950408889634

F32 = jnp.float32
BF16 = jnp.bfloat16


def _layer_norm(y, g, b):
    mu = jnp.mean(y, axis=-1, keepdims=True)
    d = y - mu
    var = jnp.mean(d * d, axis=-1, keepdims=True)
    return d * lax.rsqrt(var + LN_EPS) * g + b


def _silu(x):
    return x * jax.nn.sigmoid(x)


class _CastJob(NamedTuple):
    src: jax.Array
    col0: int
    cols: int
    transpose: bool = False


def _whole(w):
    return _CastJob(w, 0, w.shape[1])


def _cast_plan(jobs, n_steps, flat_step):
    in_specs, out_specs, out_shapes = [], [], []
    for job in jobs:
        rows = job.src.shape[0]
        align = LANES if job.transpose else BF16_SUBLANES
        chunk = align
        while rows % chunk or rows // chunk > n_steps:
            chunk += align
        last = rows // chunk - 1
        col_block = job.col0 // job.cols
        assert col_block * job.cols == job.col0

        def chunk_of(*g, last=last):
            return jnp.minimum(flat_step(*g), last)

        in_specs.append(pl.BlockSpec(
            (chunk, job.cols), lambda *g, f=chunk_of, cb=col_block: (f(*g), cb)))
        if job.transpose:
            out_specs.append(pl.BlockSpec((job.cols, chunk), lambda *g, f=chunk_of: (0, f(*g))))
            out_shapes.append(jax.ShapeDtypeStruct((job.cols, rows), BF16))
        else:
            out_specs.append(pl.BlockSpec((chunk, job.cols), lambda *g, f=chunk_of: (f(*g), 0)))
            out_shapes.append(jax.ShapeDtypeStruct((rows, job.cols), BF16))
    return in_specs, out_specs, out_shapes


def _run_cast_jobs(src_refs, dst_refs, transposes):
    for src_ref, dst_ref, transpose in zip(src_refs, dst_refs, transposes):
        w = src_ref[...]
        dst_ref[...] = (w.T if transpose else w).astype(BF16)


def _ada_kernel(*refs, transposes):
    n_cast = len(transposes)
    c_ref, w_ref, b_ref = refs[:3]
    o_ref = refs[3 + n_cast]
    _run_cast_jobs(refs[3:3 + n_cast], refs[4 + n_cast:], transposes)
    cond = _silu(c_ref[...]).astype(BF16)
    o_ref[0] = jnp.dot(cond, w_ref[...].astype(BF16), preferred_element_type=F32) + b_ref[...]


def _ada_mod(c, w_ada, b_ada, cast_jobs=()):
    bsz, d = c.shape
    cast_in, cast_out, cast_shapes = _cast_plan(cast_jobs, N_MOD, lambda j: j)
    outs = pl.pallas_call(
        functools.partial(_ada_kernel, transposes=tuple(j.transpose for j in cast_jobs)),
        grid=(N_MOD,),
        in_specs=[
            pl.BlockSpec((bsz, d), lambda j: (0, 0)),
            pl.BlockSpec((d, d), lambda j: (0, j)),
            pl.BlockSpec((1, d), lambda j: (0, j)),
        ] + cast_in,
        out_specs=[pl.BlockSpec((1, bsz, d), lambda j: (j, 0, 0))] + cast_out,
        out_shape=[jax.ShapeDtypeStruct((N_MOD, bsz, d), F32)] + cast_shapes,
        compiler_params=pltpu.CompilerParams(
            dimension_semantics=("arbitrary",), vmem_limit_bytes=VMEM_LIMIT_BYTES),
        name="ada_mod",
    )(c, w_ada, b_ada.reshape(1, N_MOD * d), *[j.src for j in cast_jobs])
    return outs[0], tuple(outs[1:])


def _ffn_kernel(*refs, transposes, mod_base, alpha):
    n_cast = len(transposes)
    x_ref, mod_ref, wgu_ref, wd_ref, g_ref, b_ref = refs[:6]
    o_ref = refs[6 + n_cast]
    scratch = refs[7 + 2 * n_cast:]
    _run_cast_jobs(refs[6:6 + n_cast], refs[7 + n_cast:7 + 2 * n_cast], transposes)

    d_ff = wd_ref.shape[0]
    tf = FFN_CHUNK
    n_sub = len(scratch) // 2
    batch = pl.ds(pl.program_id(0), 1)
    sh = mod_ref[mod_base, batch, :]
    sc = mod_ref[mod_base + 1, batch, :]
    gate = mod_ref[mod_base + 2, batch, :]
    h_ref, act_ref = scratch
    tm = h_ref.shape[0]
    h_ref[...] = (x_ref[0] * (1.0 + sc) + sh).astype(BF16)

    def gate_up(rows, c):
        lo = c * tf
        g = jnp.dot(h_ref[rows, :], wgu_ref[:, lo:lo + tf], preferred_element_type=F32)
        u = jnp.dot(h_ref[rows, :], wgu_ref[:, d_ff + lo:d_ff + lo + tf],
                    preferred_element_type=F32)
        act_ref[rows, lo:lo + tf] = (_silu(g) * u).astype(BF16)

    gate_up(slice(0, tm), 0)
    row0 = 0
    for sub in FFN_SUB_ROWS:
        rows = slice(row0, row0 + sub)
        row0 = rows.stop
        for c in range(1, d_ff // tf):
            gate_up(rows, c)
        pieces = DOWN_PIECES
        step = sub // pieces
        for p in range(pieces):
            part = slice(rows.start + p * step, rows.start + (p + 1) * step)
            down = jnp.dot(act_ref[part, :], wd_ref[...], preferred_element_type=F32)
            y = alpha * x_ref[0, part, :] + (0.5 * (1.0 + gate)) * down
            o_ref[0, part, :] = _layer_norm(y, g_ref[...], b_ref[...])


def _ffn_block(x, mod, w_gate_up, w_down, ln_g, ln_b, *, mod_base, alpha, cast_jobs=()):
    bsz, seq, d = x.shape
    tm = FFN_ROWS
    per_seq = seq // tm
    const2 = lambda b, s: (0, 0)
    cast_in, cast_out, cast_shapes = _cast_plan(
        cast_jobs, bsz * per_seq, lambda b, s: b * per_seq + s)
    outs = pl.pallas_call(
        functools.partial(_ffn_kernel, transposes=tuple(j.transpose for j in cast_jobs),
                          mod_base=mod_base, alpha=alpha),
        grid=(bsz, per_seq),
        in_specs=[
            pl.BlockSpec((1, tm, d), lambda b, s: (b, s, 0)),
            pl.BlockSpec((N_MOD, bsz, d), lambda b, s: (0, 0, 0)),
            pl.BlockSpec((d, 2 * D_FF), const2, pipeline_mode=pl.Buffered(1)),
            pl.BlockSpec((D_FF, d), const2, pipeline_mode=pl.Buffered(1)),
            pl.BlockSpec((1, d), const2),
            pl.BlockSpec((1, d), const2),
        ] + cast_in,
        out_specs=[pl.BlockSpec((1, tm, d), lambda b, s: (b, s, 0))] + cast_out,
        out_shape=[jax.ShapeDtypeStruct((bsz, seq, d), F32)] + cast_shapes,
        scratch_shapes=[pltpu.VMEM((tm, d), BF16), pltpu.VMEM((tm, D_FF), BF16)],
        compiler_params=pltpu.CompilerParams(
            dimension_semantics=("arbitrary", "arbitrary"),
            vmem_limit_bytes=VMEM_LIMIT_BYTES),
        name="ffn_block",
    )(x, mod, w_gate_up, w_down, ln_g.reshape(1, d), ln_b.reshape(1, d),
      *[j.src for j in cast_jobs])
    return outs[0], tuple(outs[1:])


def _mixer_kernel(x_ref, mod_ref, pos_ref, invf_ref, sink_ref, wqkv_ref, wconv_a_ref, wconv_b_ref,
                  cw_ref, wout_ref, g_ref, b_ref, o_ref,
                  bias_ref, k_ref, vt_ref, z_ref, hall_ref, ptall_ref, *sub_scratch, alpha):
    ts = x_ref.shape[1]
    n_sub = len(sub_scratch) // MIX_SUB_BUFFERS
    sub = ts // n_sub
    s_idx = pl.program_id(1)
    zpad = SUBLANES
    n_slabs = CONV_WIDTH // LANES

    @pl.when(s_idx == 0)
    def _():
        k_ref[0:WINDOW, :] = jnp.zeros((WINDOW, KV_WIDTH), BF16)
        vt_ref[:, 0:WINDOW] = jnp.zeros((KV_WIDTH, WINDOW), BF16)
        z_ref[:, 0:zpad, :] = jnp.zeros((n_slabs, zpad, LANES), F32)

    batch = pl.ds(pl.program_id(0), 1)
    sh = mod_ref[3, batch, :]
    sc = mod_ref[4, batch, :]
    gate = mod_ref[5, batch, :]
    contract_last = (((1,), (1,)), ((), ()))
    half = ROT_DIM // 2
    scale = HEAD_DIM ** -0.5 * LOG2_E

    rows2 = 2 * WINDOW
    no_prev = jnp.where(s_idx == 0, WINDOW, 0)
    first_pen = jnp.where(lax.broadcasted_iota(jnp.int32, (rows2, 1), 0) < no_prev, NEG_BIG, 0.0)
    sink = jnp.concatenate([jnp.full((1, WINDOW), sink_ref[hd], F32)
                            for hd in range(N_Q_HEADS)], axis=1) * LOG2_E

    @pl.when((pl.program_id(0) == 0) & (s_idx == 0))
    def _():
        ki = lax.broadcasted_iota(jnp.int32, bias_ref.shape, 0)
        qi = lax.broadcasted_iota(jnp.int32, bias_ref.shape, 1) & (WINDOW - 1)
        bias_ref[...] = jnp.where((ki > qi) & (ki <= qi + WINDOW), 0.0, NEG_BIG)

    hall_ref[...] = (x_ref[0] * (1.0 + sc) + sh).astype(BF16)
    ptall_ref[...] = lax.dot_general(wqkv_ref[...], hall_ref[...], contract_last,
                                     preferred_element_type=F32)

    for r in range(n_sub):
        qbd_ref, at_ref, pc_ref = sub_scratch[
            r * MIX_SUB_BUFFERS:(r + 1) * MIX_SUB_BUFFERS]
        t0 = r * sub
        x = x_ref[0, t0:t0 + sub, :]
        h_ref = hall_ref.at[t0:t0 + sub, :]
        pt_ref = ptall_ref.at[:, t0:t0 + sub]

        ang = invf_ref[...] * pos_ref[batch, t0:t0 + sub].astype(F32)
        cos_t = jnp.cos(ang)
        sin_t = jnp.sin(ang)

        def rope(head):
            x1 = head[0:half]
            x2 = head[half:ROT_DIM]
            return jnp.concatenate(
                [x1 * cos_t - x2 * sin_t, x2 * cos_t + x1 * sin_t, head[ROT_DIM:]], axis=0)

        qbd_ref[...] = jnp.zeros(qbd_ref.shape, BF16)
        for hd in range(N_Q_HEADS):
            kv = hd // (N_Q_HEADS // N_KV_HEADS)
            qh = (rope(pt_ref[hd * HEAD_DIM:(hd + 1) * HEAD_DIM, :]) * scale).astype(BF16)
            for n in range(sub // WINDOW):
                qbd_ref[n, kv * HEAD_DIM:(kv + 1) * HEAD_DIM, hd * WINDOW:(hd + 1) * WINDOW] = (
                    qh[:, n * WINDOW:(n + 1) * WINDOW])
        kt = jnp.concatenate(
            [rope(pt_ref[ATTN_WIDTH + kv * HEAD_DIM:ATTN_WIDTH + (kv + 1) * HEAD_DIM, :])
             for kv in range(N_KV_HEADS)], axis=0)
        k_ref[WINDOW + t0:WINDOW + t0 + sub, :] = kt.T.astype(BF16)
        vt_ref[:, WINDOW + t0:WINDOW + t0 + sub] = pt_ref[
            ATTN_WIDTH + KV_WIDTH:ATTN_WIDTH + 2 * KV_WIDTH, :].astype(BF16)

        def conv_proj(c, h_ref=h_ref, pc_ref=pc_ref):
            per_half = wconv_a_ref.shape[1] // MIX_CONV_CHUNK
            w_ref = wconv_a_ref if c < per_half else wconv_b_ref
            src = slice((c % per_half) * MIX_CONV_CHUNK, (c % per_half + 1) * MIX_CONV_CHUNK)
            dst = slice(c * MIX_CONV_CHUNK, (c + 1) * MIX_CONV_CHUNK)
            pc_ref[:, dst] = jnp.dot(h_ref[...], w_ref[:, src], preferred_element_type=F32)

        def scores(n, qbd_ref=qbd_ref, t0=t0):
            r0 = t0 + n * WINDOW
            return jnp.dot(k_ref[r0:r0 + rows2, :], qbd_ref[n], preferred_element_type=F32)

        def attend(n, sco, at_ref=at_ref, t0=t0):
            r0 = t0 + n * WINDOW
            sco = sco + bias_ref[...]
            if r0 == 0:
                sco = sco + first_pen
            mx = jnp.maximum(jnp.max(sco, axis=0, keepdims=True), sink)
            pr = jnp.exp2(sco - mx)
            den = jnp.sum(pr, axis=0, keepdims=True) + jnp.exp2(sink - mx)
            pv = jnp.dot(vt_ref[:, r0:r0 + rows2], pr.astype(BF16), preferred_element_type=F32)
            pv = pv * (1.0 / den)
            for pair in range(N_Q_HEADS // 2):
                kv = 2 * pair // (N_Q_HEADS // N_KV_HEADS)
                feat = pv[kv * HEAD_DIM:(kv + 1) * HEAD_DIM, :]
                blk = jnp.concatenate(
                    [feat[:, (2 * pair) * WINDOW:(2 * pair + 1) * WINDOW],
                     feat[:, (2 * pair + 1) * WINDOW:(2 * pair + 2) * WINDOW]], axis=0)
                at_ref[n * WINDOW:(n + 1) * WINDOW, pair * LANES:(pair + 1) * LANES] = (
                    blk.T.astype(BF16))

        n_blocks = sub // WINDOW
        n_chunks = 3 * CONV_WIDTH // MIX_CONV_CHUNK
        chunk = 0
        for n in range(n_blocks):
            pending = scores(n)
            conv_proj(chunk)
            chunk += 1
            attend(n, pending)
        while chunk < n_chunks:
            conv_proj(chunk)
            chunk += 1
        pieces = LAST_PIECES if r == n_sub - 1 else 1
        part_rows = sub // pieces

        z0 = zpad + t0
        for j in range(n_slabs):
            lanes = slice(j * LANES, (j + 1) * LANES)
            u = pc_ref[:, j * LANES:(j + 1) * LANES]
            b_gate = pc_ref[:, CONV_WIDTH + j * LANES:CONV_WIDTH + (j + 1) * LANES]
            c_gate = pc_ref[:, 2 * CONV_WIDTH + j * LANES:2 * CONV_WIDTH + (j + 1) * LANES]
            z_ref[j, z0:z0 + sub, :] = c_gate * u
            y = cw_ref[0:1, lanes] * z_ref[j, z0 - 2:z0 - 2 + sub, :]
            y = y + cw_ref[1:2, lanes] * z_ref[j, z0 - 1:z0 - 1 + sub, :]
            y = y + cw_ref[2:3, lanes] * z_ref[j, z0:z0 + sub, :]
            at_ref[:, ATTN_WIDTH + j * LANES:ATTN_WIDTH + (j + 1) * LANES] = (
                b_gate * y).astype(BF16)

        for p in range(pieces):
            part = slice(p * part_rows, (p + 1) * part_rows)
            mix = jnp.dot(at_ref[part, :], wout_ref[...], preferred_element_type=F32)
            o_ref[0, t0 + part.start:t0 + part.stop, :] = _layer_norm(
                alpha * x[part] + (1.0 + gate) * mix, g_ref[...], b_ref[...])

    k_ref[0:WINDOW, :] = k_ref[ts:ts + WINDOW, :]
    vt_ref[:, 0:WINDOW] = vt_ref[:, ts:ts + WINDOW]
    z_ref[:, 0:zpad, :] = z_ref[:, ts:ts + zpad, :]


def _mixer_block(x, mod, positions, w_qkv_t, w_conv_a, w_conv_b, conv_w, sinks, w_out, ln_g, ln_b,
                 *, alpha):
    bsz, seq, d = x.shape
    ts = MIX_ROWS
    sub = MIX_SUB_ROWS
    qkv = ATTN_WIDTH + 2 * KV_WIDTH
    conv_half = 3 * CONV_WIDTH // 2
    inv_freq = np.power(np.float32(ROPE_THETA),
                        -np.arange(0, ROT_DIM, 2, dtype=np.float32) / ROT_DIM).astype(np.float32)
    const2 = lambda b, s: (0, 0)
    sub_scratch = [
        pltpu.VMEM((sub // WINDOW, KV_WIDTH, N_Q_HEADS * WINDOW), BF16),
        pltpu.VMEM((sub, d), BF16),
        pltpu.VMEM((sub, 3 * CONV_WIDTH), F32),
    ]
    assert len(sub_scratch) == MIX_SUB_BUFFERS
    return pl.pallas_call(
        functools.partial(_mixer_kernel, alpha=alpha),
        grid=(bsz, seq // ts),
        in_specs=[
            pl.BlockSpec((1, ts, d), lambda b, s: (b, s, 0)),
            pl.BlockSpec((N_MOD, bsz, d), lambda b, s: (0, 0, 0)),
            pl.BlockSpec((bsz, ts), lambda b, s: (0, s)),
            pl.BlockSpec((ROT_DIM // 2, 1), const2),
            pl.BlockSpec(memory_space=pltpu.SMEM),
            pl.BlockSpec((qkv, d), const2, pipeline_mode=pl.Buffered(1)),
            pl.BlockSpec((d, conv_half), const2, pipeline_mode=pl.Buffered(1)),
            pl.BlockSpec((d, conv_half), const2, pipeline_mode=pl.Buffered(1)),
            pl.BlockSpec((CONV_W, CONV_WIDTH), const2),
            pl.BlockSpec((d, d), const2, pipeline_mode=pl.Buffered(1)),
            pl.BlockSpec((1, d), const2),
            pl.BlockSpec((1, d), const2),
        ],
        out_specs=pl.BlockSpec((1, ts, d), lambda b, s: (b, s, 0)),
        out_shape=jax.ShapeDtypeStruct((bsz, seq, d), F32),
        scratch_shapes=[
            pltpu.VMEM((2 * WINDOW, N_Q_HEADS * WINDOW), F32),
            pltpu.VMEM((ts + WINDOW, KV_WIDTH), BF16),
            pltpu.VMEM((KV_WIDTH, ts + WINDOW), BF16),
            pltpu.VMEM((CONV_WIDTH // LANES, ts + SUBLANES, LANES), F32),
            pltpu.VMEM((ts, d), BF16),
            pltpu.VMEM((qkv, ts), F32),
        ] + sub_scratch * (ts // sub),
        compiler_params=pltpu.CompilerParams(
            dimension_semantics=("arbitrary", "arbitrary"),
            vmem_limit_bytes=VMEM_LIMIT_BYTES),
        name="mixer_block",
    )(x, mod, positions, inv_freq.reshape(ROT_DIM // 2, 1),
      sinks.astype(F32), w_qkv_t, w_conv_a, w_conv_b, conv_w, w_out, ln_g.reshape(1, d),
      ln_b.reshape(1, d))


def kernel(x, c, positions, w_ada, b_ada, ffn1_w_gate_up, ffn1_w_down, ln1_g, ln1_b, w_in, conv_w, attn_sinks, w_out, ln2_g, ln2_b, ffn2_w_gate_up, ffn2_w_down, ln3_g, ln3_b):
    depth = w_ada.shape[0]
    alpha = (2.0 * depth) ** 0.25
    qkv = ATTN_WIDTH + 2 * KV_WIDTH
    ffn1_w = None
    for l in range(depth):
        first = () if ffn1_w else (_whole(ffn1_w_gate_up[l]), _whole(ffn1_w_down[l]))
        mod, cast = _ada_mod(c, w_ada[l], b_ada[l], cast_jobs=first)
        ffn1_w = ffn1_w or cast
        x, (w_qkv_t, w_conv_a, w_conv_b, w_out_b, ffn2_wgu, ffn2_wd) = _ffn_block(
            x, mod, *ffn1_w, ln1_g[l], ln1_b[l], mod_base=0, alpha=alpha,
            cast_jobs=(_CastJob(w_in[l], 0, qkv, transpose=True),
                       _CastJob(w_in[l], qkv, qkv), _CastJob(w_in[l], 2 * qkv, qkv),
                       _whole(w_out[l]), _whole(ffn2_w_gate_up[l]), _whole(ffn2_w_down[l])))
        x = _mixer_block(x, mod, positions, w_qkv_t, w_conv_a, w_conv_b, conv_w[l], attn_sinks[l],
                         w_out_b, ln2_g[l], ln2_b[l], alpha=alpha)
        next_ffn1 = ((_whole(ffn1_w_gate_up[l + 1]), _whole(ffn1_w_down[l + 1]))
                     if l + 1 < depth else ())
        x, ffn1_w = _ffn_block(x, mod, ffn2_wgu, ffn2_wd, ln3_g[l], ln3_b[l],
                               mod_base=6, alpha=alpha, cast_jobs=next_ffn1)
    return x
```

```python
import functools
from typing import NamedTuple

import jax
import jax.numpy as jnp
import numpy as np
from jax import lax
from jax.experimental import pallas as pl
from jax.experimental.pallas import tpu as pltpu

D_MODEL = 1024
HEAD_DIM = 64
ATTN_WIDTH = 512
CONV_WIDTH = 512
N_Q_HEADS = 8
N_KV_HEADS = 2
KV_WIDTH = N_KV_HEADS * HEAD_DIM
WINDOW = 128
ROT_DIM = 16
ROPE_THETA = 500000.0
CONV_W = 3
D_FF = 2816
N_MOD = 9
LN_EPS = 1e-5
IN_WIDTH = ATTN_WIDTH + 2 * KV_WIDTH + 3 * CONV_WIDTH

LANES = 128
SUBLANES = 8
BF16_SUBLANES = 16
VMEM_LIMIT_BYTES = 56 * 1024 * 1024

FFN_ROWS = 1024
FFN_SUB_ROWS = (512, 512)
FFN_CHUNK = 256
LAST_PIECES = 2
DOWN_PIECES = 2
MIX_ROWS = 1024
MIX_SUB_ROWS = 512
MIX_SUB_BUFFERS = 3
MIX_CONV_CHUNK = 256
NEG_BIG = -1e30
LOG2_E = 1.4426950408889634

F32 = jnp.float32
BF16 = jnp.bfloat16


def _layer_norm(y, g, b):
    mu = jnp.mean(y, axis=-1, keepdims=True)
    d = y - mu
    var = jnp.mean(d * d, axis=-1, keepdims=True)
    return d * lax.rsqrt(var + LN_EPS) * g + b


def _silu(x):
    return x * jax.nn.sigmoid(x)


class _CastJob(NamedTuple):
    src: jax.Array
    col0: int
    cols: int
    transpose: bool = False


def _whole(w):
    return _CastJob(w, 0, w.shape[1])


def _cast_plan(jobs, n_steps, flat_step):
    in_specs, out_specs, out_shapes = [], [], []
    for job in jobs:
        rows = job.src.shape[0]
        align = LANES if job.transpose else BF16_SUBLANES
        chunk = align
        while rows % chunk or rows // chunk > n_steps:
            chunk += align
        last = rows // chunk - 1
        col_block = job.col0 // job.cols
        assert col_block * job.cols == job.col0

        def chunk_of(*g, last=last):
            return jnp.minimum(flat_step(*g), last)

        in_specs.append(pl.BlockSpec(
            (chunk, job.cols), lambda *g, f=chunk_of, cb=col_block: (f(*g), cb)))
        if job.transpose:
            out_specs.append(pl.BlockSpec((job.cols, chunk), lambda *g, f=chunk_of: (0, f(*g))))
            out_shapes.append(jax.ShapeDtypeStruct((job.cols, rows), BF16))
        else:
            out_specs.append(pl.BlockSpec((chunk, job.cols), lambda *g, f=chunk_of: (f(*g), 0)))
            out_shapes.append(jax.ShapeDtypeStruct((rows, job.cols), BF16))
    return in_specs, out_specs, out_shapes


def _run_cast_jobs(src_refs, dst_refs, transposes):
    for src_ref, dst_ref, transpose in zip(src_refs, dst_refs, transposes):
        w = src_ref[...]
        dst_ref[...] = (w.T if transpose else w).astype(BF16)


def _ada_kernel(*refs, transposes):
    n_cast = len(transposes)
    c_ref, w_ref, b_ref = refs[:3]
    o_ref = refs[3 + n_cast]
    _run_cast_jobs(refs[3:3 + n_cast], refs[4 + n_cast:], transposes)
    cond = _silu(c_ref[...]).astype(BF16)
    o_ref[0] = jnp.dot(cond, w_ref[...].astype(BF16), preferred_element_type=F32) + b_ref[...]


def _ada_mod(c, w_ada, b_ada, cast_jobs=()):
    bsz, d = c.shape
    cast_in, cast_out, cast_shapes = _cast_plan(cast_jobs, N_MOD, lambda j: j)
    outs = pl.pallas_call(
        functools.partial(_ada_kernel, transposes=tuple(j.transpose for j in cast_jobs)),
        grid=(N_MOD,),
        in_specs=[
            pl.BlockSpec((bsz, d), lambda j: (0, 0)),
            pl.BlockSpec((d, d), lambda j: (0, j)),
            pl.BlockSpec((1, d), lambda j: (0, j)),
        ] + cast_in,
        out_specs=[pl.BlockSpec((1, bsz, d), lambda j: (j, 0, 0))] + cast_out,
        out_shape=[jax.ShapeDtypeStruct((N_MOD, bsz, d), F32)] + cast_shapes,
        compiler_params=pltpu.CompilerParams(
            dimension_semantics=("arbitrary",), vmem_limit_bytes=VMEM_LIMIT_BYTES),
        name="ada_mod",
    )(c, w_ada, b_ada.reshape(1, N_MOD * d), *[j.src for j in cast_jobs])
    return outs[0], tuple(outs[1:])


def _ffn_kernel(*refs, transposes, mod_base, alpha):
    n_cast = len(transposes)
    x_ref, mod_ref, wgu_ref, wd_ref, g_ref, b_ref = refs[:6]
    o_ref = refs[6 + n_cast]
    scratch = refs[7 + 2 * n_cast:]
    _run_cast_jobs(refs[6:6 + n_cast], refs[7 + n_cast:7 + 2 * n_cast], transposes)

    d_ff = wd_ref.shape[0]
    tf = FFN_CHUNK
    n_sub = len(scratch) // 2
    batch = pl.ds(pl.program_id(0), 1)
    sh = mod_ref[mod_base, batch, :]
    sc = mod_ref[mod_base + 1, batch, :]
    gate = mod_ref[mod_base + 2, batch, :]
    h_ref, act_ref = scratch
    tm = h_ref.shape[0]
    h_ref[...] = (x_ref[0] * (1.0 + sc) + sh).astype(BF16)

    def gate_up(rows, c):
        lo = c * tf
        g = jnp.dot(h_ref[rows, :], wgu_ref[:, lo:lo + tf], preferred_element_type=F32)
        u = jnp.dot(h_ref[rows, :], wgu_ref[:, d_ff + lo:d_ff + lo + tf],
                    preferred_element_type=F32)
        act_ref[rows, lo:lo + tf] = (_silu(g) * u).astype(BF16)

    gate_up(slice(0, tm), 0)
    row0 = 0
    for sub in FFN_SUB_ROWS:
        rows = slice(row0, row0 + sub)
        row0 = rows.stop
        for c in range(1, d_ff // tf):
            gate_up(rows, c)
        pieces = DOWN_PIECES
        step = sub // pieces
        for p in range(pieces):
            part = slice(rows.start + p * step, rows.start + (p + 1) * step)
            down = jnp.dot(act_ref[part, :], wd_ref[...], preferred_element_type=F32)
            y = alpha * x_ref[0, part, :] + (0.5 * (1.0 + gate)) * down
            o_ref[0, part, :] = _layer_norm(y, g_ref[...], b_ref[...])


def _ffn_block(x, mod, w_gate_up, w_down, ln_g, ln_b, *, mod_base, alpha, cast_jobs=()):
    bsz, seq, d = x.shape
    tm = FFN_ROWS
    per_seq = seq // tm
    const2 = lambda b, s: (0, 0)
    cast_in, cast_out, cast_shapes = _cast_plan(
        cast_jobs, bsz * per_seq, lambda b, s: b * per_seq + s)
    outs = pl.pallas_call(
        functools.partial(_ffn_kernel, transposes=tuple(j.transpose for j in cast_jobs),
                          mod_base=mod_base, alpha=alpha),
        grid=(bsz, per_seq),
        in_specs=[
            pl.BlockSpec((1, tm, d), lambda b, s: (b, s, 0)),
            pl.BlockSpec((N_MOD, bsz, d), lambda b, s: (0, 0, 0)),
            pl.BlockSpec((d, 2 * D_FF), const2, pipeline_mode=pl.Buffered(1)),
            pl.BlockSpec((D_FF, d), const2, pipeline_mode=pl.Buffered(1)),
            pl.BlockSpec((1, d), const2),
            pl.BlockSpec((1, d), const2),
        ] + cast_in,
        out_specs=[pl.BlockSpec((1, tm, d), lambda b, s: (b, s, 0))] + cast_out,
        out_shape=[jax.ShapeDtypeStruct((bsz, seq, d), F32)] + cast_shapes,
        scratch_shapes=[pltpu.VMEM((tm, d), BF16), pltpu.VMEM((tm, D_FF), BF16)],
        compiler_params=pltpu.CompilerParams(
            dimension_semantics=("arbitrary", "arbitrary"),
            vmem_limit_bytes=VMEM_LIMIT_BYTES),
        name="ffn_block",
    )(x, mod, w_gate_up, w_down, ln_g.reshape(1, d), ln_b.reshape(1, d),
      *[j.src for j in cast_jobs])
    return outs[0], tuple(outs[1:])


def _mixer_kernel(x_ref, mod_ref, pos_ref, invf_ref, sink_ref, wqkv_ref, wconv_a_ref, wconv_b_ref,
                  cw_ref, wout_ref, g_ref, b_ref, o_ref,
                  bias_ref, k_ref, vt_ref, z_ref, hall_ref, ptall_ref, *sub_scratch, alpha):
    ts = x_ref.shape[1]
    n_sub = len(sub_scratch) // MIX_SUB_BUFFERS
    sub = ts // n_sub
    s_idx = pl.program_id(1)
    zpad = SUBLANES
    n_slabs = CONV_WIDTH // LANES

    @pl.when(s_idx == 0)
    def _():
        k_ref[0:WINDOW, :] = jnp.zeros((WINDOW, KV_WIDTH), BF16)
        vt_ref[:, 0:WINDOW] = jnp.zeros((KV_WIDTH, WINDOW), BF16)
        z_ref[:, 0:zpad, :] = jnp.zeros((n_slabs, zpad, LANES), F32)

    batch = pl.ds(pl.program_id(0), 1)
    sh = mod_ref[3, batch, :]
    sc = mod_ref[4, batch, :]
    gate = mod_ref[5, batch, :]
    contract_last = (((1,), (1,)), ((), ()))
    half = ROT_DIM // 2
    scale = HEAD_DIM ** -0.5 * LOG2_E

    rows2 = 2 * WINDOW
    no_prev = jnp.where(s_idx == 0, WINDOW, 0)
    first_pen = jnp.where(lax.broadcasted_iota(jnp.int32, (rows2, 1), 0) < no_prev, NEG_BIG, 0.0)
    sink = jnp.concatenate([jnp.full((1, WINDOW), sink_ref[hd], F32)
                            for hd in range(N_Q_HEADS)], axis=1) * LOG2_E

    @pl.when((pl.program_id(0) == 0) & (s_idx == 0))
    def _():
        ki = lax.broadcasted_iota(jnp.int32, bias_ref.shape, 0)
        qi = lax.broadcasted_iota(jnp.int32, bias_ref.shape, 1) & (WINDOW - 1)
        bias_ref[...] = jnp.where((ki > qi) & (ki <= qi + WINDOW), 0.0, NEG_BIG)

    hall_ref[...] = (x_ref[0] * (1.0 + sc) + sh).astype(BF16)
    ptall_ref[...] = lax.dot_general(wqkv_ref[...], hall_ref[...], contract_last,
                                     preferred_element_type=F32)

    for r in range(n_sub):
        qbd_ref, at_ref, pc_ref = sub_scratch[
            r * MIX_SUB_BUFFERS:(r + 1) * MIX_SUB_BUFFERS]
        t0 = r * sub
        x = x_ref[0, t0:t0 + sub, :]
        h_ref = hall_ref.at[t0:t0 + sub, :]
        pt_ref = ptall_ref.at[:, t0:t0 + sub]

        ang = invf_ref[...] * pos_ref[batch, t0:t0 + sub].astype(F32)
        cos_t = jnp.cos(ang)
        sin_t = jnp.sin(ang)

        def rope(head):
            x1 = head[0:half]
            x2 = head[half:ROT_DIM]
            return jnp.concatenate(
                [x1 * cos_t - x2 * sin_t, x2 * cos_t + x1 * sin_t, head[ROT_DIM:]], axis=0)

        qbd_ref[...] = jnp.zeros(qbd_ref.shape, BF16)
        for hd in range(N_Q_HEADS):
            kv = hd // (N_Q_HEADS // N_KV_HEADS)
            qh = (rope(pt_ref[hd * HEAD_DIM:(hd + 1) * HEAD_DIM, :]) * scale).astype(BF16)
            for n in range(sub // WINDOW):
                qbd_ref[n, kv * HEAD_DIM:(kv + 1) * HEAD_DIM, hd * WINDOW:(hd + 1) * WINDOW] = (
                    qh[:, n * WINDOW:(n + 1) * WINDOW])
        kt = jnp.concatenate(
            [rope(pt_ref[ATTN_WIDTH + kv * HEAD_DIM:ATTN_WIDTH + (kv + 1) * HEAD_DIM, :])
             for kv in range(N_KV_HEADS)], axis=0)
        k_ref[WINDOW + t0:WINDOW + t0 + sub, :] = kt.T.astype(BF16)
        vt_ref[:, WINDOW + t0:WINDOW + t0 + sub] = pt_ref[
            ATTN_WIDTH + KV_WIDTH:ATTN_WIDTH + 2 * KV_WIDTH, :].astype(BF16)

        def conv_proj(c, h_ref=h_ref, pc_ref=pc_ref):
            per_half = wconv_a_ref.shape[1] // MIX_CONV_CHUNK
            w_ref = wconv_a_ref if c < per_half else wconv_b_ref
            src = slice((c % per_half) * MIX_CONV_CHUNK, (c % per_half + 1) * MIX_CONV_CHUNK)
            dst = slice(c * MIX_CONV_CHUNK, (c + 1) * MIX_CONV_CHUNK)
            pc_ref[:, dst] = jnp.dot(h_ref[...], w_ref[:, src], preferred_element_type=F32)

        def scores(n, qbd_ref=qbd_ref, t0=t0):
            r0 = t0 + n * WINDOW
            return jnp.dot(k_ref[r0:r0 + rows2, :], qbd_ref[n], preferred_element_type=F32)

        def attend(n, sco, at_ref=at_ref, t0=t0):
            r0 = t0 + n * WINDOW
            sco = sco + bias_ref[...]
            if r0 == 0:
                sco = sco + first_pen
            mx = jnp.maximum(jnp.max(sco, axis=0, keepdims=True), sink)
            pr = jnp.exp2(sco - mx)
            den = jnp.sum(pr, axis=0, keepdims=True) + jnp.exp2(sink - mx)
            pv = jnp.dot(vt_ref[:, r0:r0 + rows2], pr.astype(BF16), preferred_element_type=F32)
            pv = pv * (1.0 / den)
            for pair in range(N_Q_HEADS // 2):
                kv = 2 * pair // (N_Q_HEADS // N_KV_HEADS)
                feat = pv[kv * HEAD_DIM:(kv + 1) * HEAD_DIM, :]
                blk = jnp.concatenate(
                    [feat[:, (2 * pair) * WINDOW:(2 * pair + 1) * WINDOW],
                     feat[:, (2 * pair + 1) * WINDOW:(2 * pair + 2) * WINDOW]], axis=0)
                at_ref[n * WINDOW:(n + 1) * WINDOW, pair * LANES:(pair + 1) * LANES] = (
                    blk.T.astype(BF16))

        n_blocks = sub // WINDOW
        n_chunks = 3 * CONV_WIDTH // MIX_CONV_CHUNK
        chunk = 0
        for n in range(n_blocks):
            pending = scores(n)
            conv_proj(chunk)
            chunk += 1
            attend(n, pending)
        while chunk < n_chunks:
            conv_proj(chunk)
            chunk += 1
        pieces = LAST_PIECES if r == n_sub - 1 else 1
        part_rows = sub // pieces

        z0 = zpad + t0
        for j in range(n_slabs):
            lanes = slice(j * LANES, (j + 1) * LANES)
            u = pc_ref[:, j * LANES:(j + 1) * LANES]
            b_gate = pc_ref[:, CONV_WIDTH + j * LANES:CONV_WIDTH + (j + 1) * LANES]
            c_gate = pc_ref[:, 2 * CONV_WIDTH + j * LANES:2 * CONV_WIDTH + (j + 1) * LANES]
            z_ref[j, z0:z0 + sub, :] = c_gate * u
            y = cw_ref[0:1, lanes] * z_ref[j, z0 - 2:z0 - 2 + sub, :]
            y = y + cw_ref[1:2, lanes] * z_ref[j, z0 - 1:z0 - 1 + sub, :]
            y = y + cw_ref[2:3, lanes] * z_ref[j, z0:z0 + sub, :]
            at_ref[:, ATTN_WIDTH + j * LANES:ATTN_WIDTH + (j + 1) * LANES] = (
                b_gate * y).astype(BF16)

        for p in range(pieces):
            part = slice(p * part_rows, (p + 1) * part_rows)
            mix = jnp.dot(at_ref[part, :], wout_ref[...], preferred_element_type=F32)
            o_ref[0, t0 + part.start:t0 + part.stop, :] = _layer_norm(
                alpha * x[part] + (1.0 + gate) * mix, g_ref[...], b_ref[...])

    k_ref[0:WINDOW, :] = k_ref[ts:ts + WINDOW, :]
    vt_ref[:, 0:WINDOW] = vt_ref[:, ts:ts + WINDOW]
    z_ref[:, 0:zpad, :] = z_ref[:, ts:ts + zpad, :]


def _mixer_block(x, mod, positions, w_qkv_t, w_conv_a, w_conv_b, conv_w, sinks, w_out, ln_g, ln_b,
                 *, alpha):
    bsz, seq, d = x.shape
    ts = MIX_ROWS
    sub = MIX_SUB_ROWS
    qkv = ATTN_WIDTH + 2 * KV_WIDTH
    inv_freq = np.power(np.float32(ROPE_THETA),
                        -np.arange(0, ROT_DIM, 2, dtype=np.float32) / ROT_DIM).astype(np.float32)
    sub_scratch = [
        pltpu.VMEM((sub // WINDOW, KV_WIDTH, N_Q_HEADS * WINDOW), BF16),
        pltpu.VMEM((sub, d), BF16),
        pltpu.VMEM((sub, 3 * CONV_WIDTH), F32),
    ]
    assert len(sub_scratch) == MIX_SUB_BUFFERS
    tile_spec = pl.BlockSpec((1, ts, d), lambda b, s: (b, s, 0))
    pos_spec = pl.BlockSpec((bsz, ts), lambda b, s: (0, s))
    n_resident = 9

    def call_kernel(x_hbm, mod_ref, pos_hbm, *rest):
        resident, o_hbm, scratch = rest[:n_resident], rest[n_resident], rest[n_resident + 1:]

        def step(x_ref, pos_ref, o_ref):
            _mixer_kernel(x_ref, mod_ref, pos_ref, *resident, o_ref, *scratch, alpha=alpha)

        pltpu.emit_pipeline(step, grid=(bsz, seq // ts), in_specs=[tile_spec, pos_spec],
                            out_specs=[tile_spec])(x_hbm, pos_hbm, o_hbm)

    streamed = pl.BlockSpec(memory_space=pl.ANY)
    whole = pl.BlockSpec(memory_space=pltpu.VMEM)
    return pl.pallas_call(
        call_kernel,
        in_specs=[streamed, whole, streamed, whole, pl.BlockSpec(memory_space=pltpu.SMEM)]
        + [whole] * (n_resident - 2),
        out_specs=streamed,
        out_shape=jax.ShapeDtypeStruct((bsz, seq, d), F32),
        scratch_shapes=[
            pltpu.VMEM((2 * WINDOW, N_Q_HEADS * WINDOW), F32),
            pltpu.VMEM((ts + WINDOW, KV_WIDTH), BF16),
            pltpu.VMEM((KV_WIDTH, ts + WINDOW), BF16),
            pltpu.VMEM((CONV_WIDTH // LANES, ts + SUBLANES, LANES), F32),
            pltpu.VMEM((ts, d), BF16),
            pltpu.VMEM((qkv, ts), F32),
        ] + sub_scratch * (ts // sub),
        compiler_params=pltpu.CompilerParams(vmem_limit_bytes=VMEM_LIMIT_BYTES),
        name="mixer_block",
    )(x, mod, positions, inv_freq.reshape(ROT_DIM // 2, 1),
      sinks.astype(F32), w_qkv_t, w_conv_a, w_conv_b, conv_w, w_out, ln_g.reshape(1, d),
      ln_b.reshape(1, d))


def kernel(x, c, positions, w_ada, b_ada, ffn1_w_gate_up, ffn1_w_down, ln1_g, ln1_b, w_in, conv_w, attn_sinks, w_out, ln2_g, ln2_b, ffn2_w_gate_up, ffn2_w_down, ln3_g, ln3_b):
    depth = w_ada.shape[0]
    alpha = (2.0 * depth) ** 0.25
    qkv = ATTN_WIDTH + 2 * KV_WIDTH
    ffn1_w = None
    for l in range(depth):
        first = () if ffn1_w else (_whole(ffn1_w_gate_up[l]), _whole(ffn1_w_down[l]))
        mod, cast = _ada_mod(c, w_ada[l], b_ada[l], cast_jobs=first)
        ffn1_w = ffn1_w or cast
        x, (w_qkv_t, w_conv_a, w_conv_b, w_out_b, ffn2_wgu, ffn2_wd) = _ffn_block(
            x, mod, *ffn1_w, ln1_g[l], ln1_b[l], mod_base=0, alpha=alpha,
            cast_jobs=(_CastJob(w_in[l], 0, qkv, transpose=True),
                       _CastJob(w_in[l], qkv, qkv), _CastJob(w_in[l], 2 * qkv, qkv),
                       _whole(w_out[l]), _whole(ffn2_w_gate_up[l]), _whole(ffn2_w_down[l])))
        x = _mixer_block(x, mod, positions, w_qkv_t, w_conv_a, w_conv_b, conv_w[l], attn_sinks[l],
                         w_out_b, ln2_g[l], ln2_b[l], alpha=alpha)
        next_ffn1 = ((_whole(ffn1_w_gate_up[l + 1]), _whole(ffn1_w_down[l + 1]))
                     if l + 1 < depth else ())
        x, ffn1_w = _ffn_block(x, mod, ffn2_wgu, ffn2_wd, ln3_g[l], ln3_b[l],
                               mod_base=6, alpha=alpha, cast_jobs=next_ffn1)
    return x
```
